```python
import jax, jax.numpy as jnp
from jax import lax
import numpy as np

D_MODEL = 1024
BATCH = 4
SEQ = 8192
DEPTH = 1

CONV_W = D_MODEL // 2
N_HEADS = 8
HEAD_DIM = (D_MODEL - CONV_W) // N_HEADS
ATTN_W = N_HEADS * HEAD_DIM
IN_COLS = 3 * CONV_W + 3 * ATTN_W
CONV_K = 3
D_FF = 2816
PLE_DIM = 256
DILATED_PAIRS = ((128, 1), (512, 4), (2048, 16))
BLOCK = 128
EPS = 1e-6

kernel_name = 'hybrid_conv_dilated_attn_convffn_ple'


def rmsnorm(a, g):
    af = a.astype(jnp.float32)
    af = af * lax.rsqrt(jnp.mean(af * af, axis=-1, keepdims=True) + EPS)
    return (af * g.astype(jnp.float32)).astype(a.dtype)


def causal_dwconv3(u, w, b):
    up = jnp.pad(u, ((0, 0), (CONV_K - 1, 0), (0, 0)))
    t = u.shape[1]
    return up[:, 0:t] * w[0] + up[:, 1:t + 1] * w[1] + up[:, 2:t + 2] * w[2] + b


def alibi_slopes(n):
    return jnp.exp2(-8.0 * jnp.arange(1, n + 1, dtype=jnp.float32) / n)


def dilated_branch(q, k, v, slopes, window, dilation):
    b, t, h, hd = q.shape
    steps = window // dilation
    L = t // dilation
    nb = -(-L // BLOCK)
    lp = nb * BLOCK

    def to_blocks(a):
        a = a.reshape(b, L, dilation, h, hd).transpose(0, 2, 3, 1, 4)
        a = jnp.pad(a, ((0, 0), (0, 0), (0, 0), (0, lp - L), (0, 0)))
        return a.reshape(b, dilation, h, nb, BLOCK, hd)

    def with_prev(a):
        prev = jnp.pad(a[:, :, :, :-1], ((0, 0), (0, 0), (0, 0), (1, 0), (0, 0), (0, 0)))
        return jnp.concatenate([prev, a], axis=4)

    qb = to_blocks(q)
    kk = with_prev(to_blocks(k))
    vv = with_prev(to_blocks(v))

    s = jnp.einsum('brhnqd,brhnkd->brhnqk', qb, kk) * (hd ** -0.5)
    qi = jnp.arange(BLOCK)[:, None] + BLOCK
    kj = jnp.arange(2 * BLOCK)[None, :]
    step = qi - kj
    band = (step >= 0) & (step <= steps)
    first = (jnp.arange(nb)[:, None, None] == 0) & (kj < BLOCK)[None]
    mask = band[None] & ~first
    dist = (step * dilation).astype(jnp.float32)
    bias = -slopes[:, None, None] * dist[None]
    s = jnp.where(mask, s + bias[:, None], -jnp.inf)
    m = jnp.max(s, axis=-1, keepdims=True)
    e = jnp.exp(s - m)
    den = jnp.sum(e, axis=-1, keepdims=True)
    o = jnp.einsum('brhnqk,brhnkd->brhnqd', e, vv) / den
    lse = (m + jnp.log(den))[..., 0]

    o = o.reshape(b, dilation, h, lp, hd)[:, :, :, :L]
    o = o.transpose(0, 3, 1, 2, 4).reshape(b, t, h, hd)
    lse = lse.reshape(b, dilation, h, lp)[:, :, :, :L]
    lse = lse.transpose(0, 3, 1, 2).reshape(b, t, h)
    return o, lse


def dilated_attention(q, k, v):
    slopes = alibi_slopes(q.shape[2])
    outs, lses = [], []
    for window, dilation in DILATED_PAIRS:
        o, lse = dilated_branch(q, k, v, slopes, window, dilation)
        outs.append(o)
        lses.append(lse)
    wts = jax.nn.softmax(jnp.stack(lses, axis=0), axis=0)
    o = jnp.sum(wts[..., None] * jnp.stack(outs, axis=0), axis=0)
    return o


def setup_inputs(seed: int = 0) -> dict:
    key = jax.random.key(seed)
    ks = jax.random.split(key, 24)
    f32 = jnp.float32

    def nrm(k, shape, fan):
        return jax.random.normal(k, shape, f32) * (fan ** -0.5)

    def gain(k, shape):
        return 1.0 + 0.05 * jax.random.normal(k, shape, f32)

    def bias(k, shape):
        return 0.01 * jax.random.normal(k, shape, f32)

    return {
        'x': jax.random.normal(ks[0], (BATCH, SEQ, D_MODEL), f32),
        'p': jax.random.normal(ks[1], (DEPTH, BATCH, SEQ, PLE_DIM), f32),
        'g_mix': gain(ks[2], (DEPTH, D_MODEL)),
        'w_in': nrm(ks[3], (DEPTH, D_MODEL, IN_COLS), D_MODEL),
        'conv_w': nrm(ks[4], (DEPTH, CONV_K, CONV_W), CONV_K),
        'conv_b': bias(ks[5], (DEPTH, CONV_W)),
        'q_norm_g': gain(ks[6], (DEPTH, HEAD_DIM)),
        'k_norm_g': gain(ks[7], (DEPTH, HEAD_DIM)),
        'g_out_conv': gain(ks[8], (DEPTH, CONV_W)),
        'g_out_attn': gain(ks[9], (DEPTH, ATTN_W)),
        'w_out': nrm(ks[10], (DEPTH, CONV_W + ATTN_W, D_MODEL), CONV_W + ATTN_W),
        'g_ffn': gain(ks[11], (DEPTH, D_MODEL)),
        'w_gate': nrm(ks[12], (DEPTH, D_MODEL, D_FF), D_MODEL),
        'w_up': nrm(ks[13], (DEPTH, D_MODEL, D_FF), D_MODEL),
        'ffn_conv_w': nrm(ks[14], (DEPTH, CONV_K, D_FF), CONV_K),
        'ffn_conv_b': bias(ks[15], (DEPTH, D_FF)),
        'w_down': nrm(ks[16], (DEPTH, D_FF, D_MODEL), D_FF),
        'g_ple': gain(ks[17], (DEPTH, D_MODEL)),
        'w_ple_gate': nrm(ks[18], (DEPTH, D_MODEL, D_MODEL), D_MODEL),
        'w_ple_proj': nrm(ks[19], (DEPTH, PLE_DIM, D_MODEL), PLE_DIM),
    }


def reference(x, p, g_mix, w_in, conv_w, conv_b, q_norm_g, k_norm_g, g_out_conv,
              g_out_attn, w_out, g_ffn, w_gate, w_up, ffn_conv_w, ffn_conv_b, w_down,
              g_ple, w_ple_gate, w_ple_proj):
    b, t, _ = x.shape
    for i in range(DEPTH):
        h = rmsnorm(x, g_mix[i])
        z = h @ w_in[i]
        zb, zc, zx, zq, zk, zv = jnp.split(
            z, np.cumsum([CONV_W, CONV_W, CONV_W, ATTN_W, ATTN_W]), axis=-1)
        y_c = zb * causal_dwconv3(zc * zx, conv_w[i], conv_b[i])
        q = rmsnorm(zq.reshape(b, t, N_HEADS, HEAD_DIM), q_norm_g[i]).astype(jnp.float32)
        k = rmsnorm(zk.reshape(b, t, N_HEADS, HEAD_DIM), k_norm_g[i]).astype(jnp.float32)
        v = zv.reshape(b, t, N_HEADS, HEAD_DIM).astype(jnp.float32)
        y_a = dilated_attention(q, k, v).reshape(b, t, ATTN_W).astype(x.dtype)
        y = jnp.concatenate([rmsnorm(y_c, g_out_conv[i]), rmsnorm(y_a, g_out_attn[i])], axis=-1)
        x = x + y @ w_out[i]
        h = rmsnorm(x, g_ffn[i])
        gate = causal_dwconv3(h @ w_gate[i], ffn_conv_w[i], ffn_conv_b[i])
        x = x + (jax.nn.silu(gate) * (h @ w_up[i])) @ w_down[i]
        ple_gate = jax.nn.sigmoid(rmsnorm(x, g_ple[i]) @ w_ple_gate[i])
        x = x + ple_gate * (p[i].astype(x.dtype) @ w_ple_proj[i])
    return x
```

```python
import functools

import numpy as np
import jax
import jax.numpy as jnp
from jax import lax
from jax.experimental import pallas as pl
from jax.experimental.pallas import tpu as pltpu

D_MODEL = 1024
CONV_W = 512
N_HEADS = 8
HEAD_DIM = 64
ATTN_W = N_HEADS * HEAD_DIM
CONV_K = 3
D_FF = 2816
PLE_DIM = 256
DILATED_PAIRS = ((128, 1), (512, 4), (2048, 16))
BLOCK = 128
EPS = 1e-6

LANES = 128
SUBLANES = 8
HEADS_PER_SLAB = LANES // HEAD_DIM
N_SLABS = ATTN_W // LANES

TOK_TILE = 512
ATTN_TILE = 2048
FF_CHUNK = 256
N_FF_CHUNKS = D_FF // FF_CHUNK
NEG = -1e30
VMEM_LIMIT = 56 * 1024 * 1024

_F32 = jnp.float32
_BF16 = jnp.bfloat16


def _rms(x, g):
    ms = jnp.mean(x * x, axis=-1, keepdims=True)
    return x * lax.rsqrt(ms + EPS) * g


def _mm(a, b):
    return jnp.dot(a, b, preferred_element_type=_F32)


def _causal_conv3(u, w_ref, b_ref, carry_ref, work_ref, first_tile):
    n = u.shape[0]

    @pl.when(first_tile)
    def _():
        carry_ref[...] = jnp.zeros_like(carry_ref)

    work_ref[0:SUBLANES, :] = carry_ref[...]
    work_ref[SUBLANES:SUBLANES + n, :] = u
    u1 = work_ref[SUBLANES - 1:SUBLANES - 1 + n, :]
    u2 = work_ref[SUBLANES - 2:SUBLANES - 2 + n, :]
    carry_ref[...] = work_ref[n:n + SUBLANES, :]
    return u2 * w_ref[0:1, :] + u1 * w_ref[1:2, :] + u * w_ref[2:3, :] + b_ref[...]


def _proj_kernel(x_ref, gmix_ref, win_ref, cw_ref, cb_ref, gq_ref, gk_ref, goc_ref, gsum_ref,
                 yc_ref, q_ref, k_ref, v_ref, carry_ref, work_ref):
    first_tile = pl.program_id(1) == 0
    h = _rms(x_ref[0], gmix_ref[...]).astype(_BF16)

    def proj(c):
        return _mm(h, win_ref[:, c * CONV_W:(c + 1) * CONV_W])

    u = proj(1) * proj(2)
    conv = _causal_conv3(u, cw_ref, cb_ref, carry_ref, work_ref, first_tile)
    yc = proj(0) * conv
    yc_ref[0] = _rms(yc, goc_ref[...]).astype(_BF16)

    def head_norm(z, g_ref):
        ms = _mm((z * z).astype(_BF16), gsum_ref[...])
        return z * lax.rsqrt(ms + EPS) * g_ref[...]

    q_ref[0] = head_norm(proj(3), gq_ref) * (HEAD_DIM ** -0.5)
    k_ref[0] = head_norm(proj(4), gk_ref)
    v_ref[0] = proj(5)


def _proj_call(x, g_mix, w_in, conv_w, conv_b, gq, gk, g_out_conv, gsum):
    b, t, _ = x.shape
    const = lambda *shape: pl.BlockSpec(shape, lambda i, j: (0,) * len(shape))
    tile = lambda w: pl.BlockSpec((1, TOK_TILE, w), lambda i, j: (i, j, 0))
    return pl.pallas_call(
        _proj_kernel,
        grid=(b, t // TOK_TILE),
        in_specs=[tile(D_MODEL), const(1, D_MODEL), const(D_MODEL, 3 * CONV_W + 3 * ATTN_W),
                  const(CONV_K, CONV_W), const(1, CONV_W), const(1, ATTN_W), const(1, ATTN_W),
                  const(1, CONV_W), const(ATTN_W, ATTN_W)],
        out_specs=[tile(CONV_W), tile(ATTN_W), tile(ATTN_W), tile(ATTN_W)],
        out_shape=[jax.ShapeDtypeStruct((b, t, CONV_W), _BF16),
                   jax.ShapeDtypeStruct((b, t, ATTN_W), _F32),
                   jax.ShapeDtypeStruct((b, t, ATTN_W), _F32),
                   jax.ShapeDtypeStruct((b, t, ATTN_W), _F32)],
        scratch_shapes=[pltpu.VMEM((SUBLANES, CONV_W), _F32),
                        pltpu.VMEM((SUBLANES + TOK_TILE, CONV_W), _F32)],
        compiler_params=pltpu.CompilerParams(
            dimension_semantics=("arbitrary", "arbitrary"), vmem_limit_bytes=VMEM_LIMIT),
        name="proj",
    )(x, g_mix, w_in, conv_w, conv_b, gq, gk, g_out_conv, gsum)


_BRANCHES = tuple(sorted(DILATED_PAIRS, key=lambda wd: -wd[1]))


def _bias_tables():
    qi = np.arange(BLOCK)[:, None] + BLOCK
    kj = np.arange(2 * BLOCK)[None, :]
    step = (qi - kj).astype(np.float64)
    tabs = np.zeros((N_SLABS, len(_BRANCHES), HEADS_PER_SLAB, BLOCK, 2 * BLOCK), np.float32)
    for bi, (window, dilation) in enumerate(_BRANCHES):
        band = (step >= 0) & (step <= window // dilation)
        for head in range(N_HEADS):
            slope = 2.0 ** (-8.0 * (head + 1) / N_HEADS)
            tabs[head // HEADS_PER_SLAB, bi, head % HEADS_PER_SLAB] = np.where(
                band, -slope * step * dilation, NEG)
    return tabs


def _attn_kernel(q_ref, k_ref, v_ref, bias_ref, o_ref, acc_ref, m_ref, l_ref):
    base = pl.program_id(2) * ATTN_TILE
    lane = lax.broadcasted_iota(jnp.int32, (BLOCK, LANES), 1)
    head_lanes = [(lane >= h * HEAD_DIM) & (lane < (h + 1) * HEAD_DIM) for h in range(HEADS_PER_SLAB)]
    prev_cols = lax.broadcasted_iota(jnp.int32, (1, 2 * BLOCK), 1) < BLOCK

    def rows(start, d):
        return pl.ds(start, BLOCK, stride=d) if d > 1 else pl.ds(start, BLOCK)

    def block(qs, bi, d, init, final):
        gs = base + qs
        no_prev = gs < BLOCK * d
        ps = jnp.where(no_prev, gs, gs - BLOCK * d)
        pen = jnp.where(prev_cols, jnp.where(no_prev, NEG, 0.0).astype(_F32), 0.0)
        q2 = q_ref[0, rows(qs, d), :]
        kk = jnp.concatenate([k_ref[0, rows(ps, d), :], k_ref[0, rows(gs, d), :]], axis=0).astype(_BF16)
        vv = jnp.concatenate([v_ref[0, rows(ps, d), :], v_ref[0, rows(gs, d), :]], axis=0).astype(_BF16)
        acc_old = None if init else acc_ref[rows(qs, d), :]
        accs, ls = [], []
        for h in range(HEADS_PER_SLAB):
            qh = jnp.where(head_lanes[h], q2, 0.0).astype(_BF16)
            s = lax.dot_general(qh, kk, (((1,), (1,)), ((), ())), preferred_element_type=_F32)
            s = s + (bias_ref[0, bi, h] + pen)
            m_blk = jnp.broadcast_to(jnp.max(s, axis=-1, keepdims=True), (BLOCK, LANES))
            if init:
                m_new = m_blk
            else:
                m_old = m_ref[h, rows(qs, d), :]
                m_new = jnp.maximum(m_old, m_blk)
                alpha = jnp.exp(m_old - m_new)
            p = jnp.exp(s - jnp.concatenate([m_new, m_new], axis=1))
            l_new = jnp.broadcast_to(jnp.sum(p, axis=-1, keepdims=True), (BLOCK, LANES))
            acc = _mm(p.astype(_BF16), vv)
            if not init:
                l_new = alpha * l_ref[h, rows(qs, d), :] + l_new
                acc = alpha * acc_old + acc
            if not final:
                m_ref[h, rows(qs, d), :] = m_new
                l_ref[h, rows(qs, d), :] = l_new
            accs.append(acc)
            ls.append(l_new)
        if final:
            accs = [a / l for a, l in zip(accs, ls)]
        out = accs[-1]
        for h in range(HEADS_PER_SLAB - 2, -1, -1):
            out = jnp.where(head_lanes[h], accs[h], out)
        if final:
            o_ref[0, rows(qs, d), :] = out
        else:
            acc_ref[rows(qs, d), :] = out

    n_blocks = ATTN_TILE // BLOCK
    for bi, (_, d) in enumerate(_BRANCHES):
        log2_d = d.bit_length() - 1

        def body(i, carry, bi=bi, d=d, log2_d=log2_d):
            if d > 1:
                qs = lax.shift_right_logical(i, log2_d) * (BLOCK * d) + (i & (d - 1))
            else:
                qs = pl.multiple_of(i * BLOCK, BLOCK)
            block(qs, bi, d, init=(bi == 0), final=(bi == len(_BRANCHES) - 1))
            return carry

        lax.fori_loop(0, n_blocks, body, 0)


def _attn_call(q, k, v, bias):
    b, t, _ = q.shape
    return pl.pallas_call(
        _attn_kernel,
        grid=(b, N_SLABS, t // ATTN_TILE),
        in_specs=[pl.BlockSpec((1, ATTN_TILE, LANES), lambda i, j, s: (i, s, j)),
                  pl.BlockSpec((1, t, LANES), lambda i, j, s: (i, 0, j)),
                  pl.BlockSpec((1, t, LANES), lambda i, j, s: (i, 0, j)),
                  pl.BlockSpec((1,) + bias.shape[1:], lambda i, j, s: (j, 0, 0, 0, 0))],
        out_specs=pl.BlockSpec((1, ATTN_TILE, LANES), lambda i, j, s: (i, s, j)),
        out_shape=jax.ShapeDtypeStruct((b, t, ATTN_W), _F32),
        scratch_shapes=[pltpu.VMEM((ATTN_TILE, LANES), _F32),
                        pltpu.VMEM((HEADS_PER_SLAB, ATTN_TILE, LANES), _F32),
                        pltpu.VMEM((HEADS_PER_SLAB, ATTN_TILE, LANES), _F32)],
        compiler_params=pltpu.CompilerParams(
            dimension_semantics=("arbitrary", "arbitrary", "arbitrary"), vmem_limit_bytes=VMEM_LIMIT),
        name="attn",
    )(q, k, v, bias)


def _ffn_kernel(x_ref, yc_ref, ya_ref, p_ref, goa_ref, wout_ref, gffn_ref, wg_ref, wu_ref, fcw_ref, fcb_ref,
                wd_ref, gple_ref, wpg_ref, wpp_ref, o_ref, acc_ref, carry_ref, work_ref):
    first_tile = pl.program_id(1) == 0
    ya = _rms(ya_ref[0], goa_ref[...]).astype(_BF16)
    x1 = x_ref[0] + _mm(yc_ref[0], wout_ref[0:CONV_W, :]) + _mm(ya, wout_ref[CONV_W:CONV_W + ATTN_W, :])
    acc_ref[...] = x1
    h = _rms(x1, gffn_ref[...]).astype(_BF16)

    def chunk(c, carry):
        g = _mm(h, wg_ref[c])
        gate = _causal_conv3(g, fcw_ref.at[c], fcb_ref.at[c], carry_ref.at[c], work_ref, first_tile)
        act = gate / (1.0 + jnp.exp(-gate)) * _mm(h, wu_ref[c])
        acc_ref[...] += _mm(act.astype(_BF16), wd_ref[c])
        return carry

    lax.fori_loop(0, N_FF_CHUNKS, chunk, 0)
    x2 = acc_ref[...]
    h = _rms(x2, gple_ref[...]).astype(_BF16)
    gate = 1.0 / (1.0 + jnp.exp(-_mm(h, wpg_ref[...])))
    o_ref[0] = x2 + gate * _mm(p_ref[0].astype(_BF16), wpp_ref[...])


def _ffn_call(x, yc, ya, p, g_out_attn, w_out, g_ffn, wg, wu, fcw, fcb, wd, g_ple, wpg, wpp):
    b, t, _ = x.shape

    def const(*shape):
        return pl.BlockSpec(shape, lambda i, j: (0,) * len(shape), pipeline_mode=pl.Buffered(1))

    tile = lambda w: pl.BlockSpec((1, TOK_TILE, w), lambda i, j: (i, j, 0))
    return pl.pallas_call(
        _ffn_kernel,
        grid=(b, t // TOK_TILE),
        in_specs=[tile(D_MODEL), tile(CONV_W), tile(ATTN_W), tile(PLE_DIM),
                  const(1, ATTN_W), const(CONV_W + ATTN_W, D_MODEL), const(1, D_MODEL),
                  const(N_FF_CHUNKS, D_MODEL, FF_CHUNK), const(N_FF_CHUNKS, D_MODEL, FF_CHUNK),
                  const(N_FF_CHUNKS, CONV_K, FF_CHUNK), const(N_FF_CHUNKS, 1, FF_CHUNK),
                  const(N_FF_CHUNKS, FF_CHUNK, D_MODEL), const(1, D_MODEL),
                  const(D_MODEL, D_MODEL), const(PLE_DIM, D_MODEL)],
        out_specs=tile(D_MODEL),
        out_shape=jax.ShapeDtypeStruct((b, t, D_MODEL), _F32),
        scratch_shapes=[pltpu.VMEM((TOK_TILE, D_MODEL), _F32),
                        pltpu.VMEM((N_FF_CHUNKS, SUBLANES, FF_CHUNK), _F32),
                        pltpu.VMEM((SUBLANES + TOK_TILE, FF_CHUNK), _F32)],
        compiler_params=pltpu.CompilerParams(
            dimension_semantics=("arbitrary", "arbitrary"), vmem_limit_bytes=VMEM_LIMIT),
        name="ffn",
    )(x, yc, ya, p, g_out_attn, w_out, g_ffn, wg, wu, fcw, fcb, wd, g_ple, wpg, wpp)


def _chunk_cols(w):
    return w.reshape(w.shape[0], N_FF_CHUNKS, FF_CHUNK).transpose(1, 0, 2)


def kernel(x, p, g_mix, w_in, conv_w, conv_b, q_norm_g, k_norm_g, g_out_conv, g_out_attn, w_out, g_ffn,
           w_gate, w_up, ffn_conv_w, ffn_conv_b, w_down, g_ple, w_ple_gate, w_ple_proj):
    depth = w_in.shape[0]
    row = lambda a: a.reshape(1, -1)
    gsum = jnp.asarray(np.kron(np.eye(N_HEADS), np.full((HEAD_DIM, HEAD_DIM), 1.0 / HEAD_DIM)), _BF16)
    bias = jnp.asarray(_bias_tables())
    for i in range(depth):
        yc, q, k, v = _proj_call(
            x, row(g_mix[i]), w_in[i].astype(_BF16), conv_w[i], row(conv_b[i]),
            row(jnp.tile(q_norm_g[i], N_HEADS)), row(jnp.tile(k_norm_g[i], N_HEADS)),
            row(g_out_conv[i]), gsum)
        ya = _attn_call(q, k, v, bias)
        x = _ffn_call(
            x, yc, ya, p[i], row(g_out_attn[i]), w_out[i].astype(_BF16), row(g_ffn[i]),
            _chunk_cols(w_gate[i]).astype(_BF16), _chunk_cols(w_up[i]).astype(_BF16),
            _chunk_cols(ffn_conv_w[i]), _chunk_cols(row(ffn_conv_b[i])),
            w_down[i].astype(_BF16).reshape(N_FF_CHUNKS, FF_CHUNK, D_MODEL), row(g_ple[i]),
            w_ple_gate[i].astype(_BF16), w_ple_proj[i].astype(_BF16))
    return x
```

```python
import numpy as np
import jax
import jax.numpy as jnp
from jax import lax
from jax.experimental import pallas as pl
from jax.experimental.pallas import tpu as pltpu

D_MODEL = 1024
CONV_W = 512
N_HEADS = 8
HEAD_DIM = 64
ATTN_W = N_HEADS * HEAD_DIM
CONV_K = 3
D_FF = 2816
PLE_DIM = 256
DILATED_PAIRS = ((128, 1), (512, 4), (2048, 16))
BLOCK = 128
EPS = 1e-6

LANES = 128
SUBLANES = 8
HEADS_PER_SLAB = LANES // HEAD_DIM
N_SLABS = ATTN_W // LANES

TOK_TILE = 512
ATTN_TILE = 2048
ATTN_GROUP = 4
FF_CHUNK = 256
N_FF_CHUNKS = D_FF // FF_CHUNK
NEG = -1e30
LOG2E = 1.4426950408889634
VMEM_LIMIT = 56 * 1024 * 1024

_F32 = jnp.float32
_BF16 = jnp.bfloat16


def _rms(x, g):
    ms = jnp.mean(x * x, axis=-1, keepdims=True)
    return x * lax.rsqrt(ms + EPS) * g


def _mm(a, b):
    return jnp.dot(a, b, preferred_element_type=_F32)


def _causal_conv3(u, w_ref, b_ref, carry_ref, work_ref):
    n = u.shape[0]
    work_ref[0:SUBLANES, :] = carry_ref[...]
    work_ref[SUBLANES:SUBLANES + n, :] = u
    u1 = work_ref[SUBLANES - 1:SUBLANES - 1 + n, :]
    u2 = work_ref[SUBLANES - 2:SUBLANES - 2 + n, :]
    carry_ref[...] = work_ref[n:n + SUBLANES, :]
    return u2 * w_ref[0:1, :] + u1 * w_ref[1:2, :] + u * w_ref[2:3, :] + b_ref[...]


def _proj_kernel(x_ref, gmix_ref, win_ref, cw_ref, cb_ref, gq_ref, gk_ref, goc_ref, gsum_ref,
                 yc_ref, q_ref, k_ref, v_ref, carry_ref, work_ref):
    @pl.when(pl.program_id(1) == 0)
    def _():
        carry_ref[...] = jnp.zeros_like(carry_ref)

    h = _rms(x_ref[0], gmix_ref[...]).astype(_BF16)

    def proj(c):
        return _mm(h, win_ref[:, c * CONV_W:(c + 1) * CONV_W])

    u = proj(1) * proj(2)
    conv = _causal_conv3(u, cw_ref, cb_ref, carry_ref, work_ref)
    yc = proj(0) * conv
    yc_ref[0] = _rms(yc, goc_ref[...]).astype(_BF16)

    def head_norm(z, g_ref):
        ms = _mm((z * z).astype(_BF16), gsum_ref[...])
        return z * lax.rsqrt(ms + EPS) * g_ref[...]

    q_ref[0] = head_norm(proj(3), gq_ref) * (HEAD_DIM ** -0.5 * LOG2E)
    k_ref[0] = head_norm(proj(4), gk_ref)
    v_ref[0] = proj(5)


def _proj_call(x, g_mix, w_in, conv_w, conv_b, gq, gk, g_out_conv, gsum):
    b, t, _ = x.shape
    const = lambda *shape: pl.BlockSpec(shape, lambda i, j: (0,) * len(shape))
    tile = lambda w: pl.BlockSpec((1, TOK_TILE, w), lambda i, j: (i, j, 0))
    return pl.pallas_call(
        _proj_kernel,
        grid=(b, t // TOK_TILE),
        in_specs=[tile(D_MODEL), const(1, D_MODEL), const(D_MODEL, 3 * CONV_W + 3 * ATTN_W),
                  const(CONV_K, CONV_W), const(1, CONV_W), const(1, ATTN_W), const(1, ATTN_W),
                  const(1, CONV_W), const(ATTN_W, ATTN_W)],
        out_specs=[tile(CONV_W), tile(ATTN_W), tile(ATTN_W), tile(ATTN_W)],
        out_shape=[jax.ShapeDtypeStruct((b, t, CONV_W), _BF16),
                   jax.ShapeDtypeStruct((b, t, ATTN_W), _F32),
                   jax.ShapeDtypeStruct((b, t, ATTN_W), _F32),
                   jax.ShapeDtypeStruct((b, t, ATTN_W), _F32)],
        scratch_shapes=[pltpu.VMEM((SUBLANES, CONV_W), _F32),
                        pltpu.VMEM((SUBLANES + TOK_TILE, CONV_W), _F32)],
        compiler_params=pltpu.CompilerParams(
            dimension_semantics=("arbitrary", "arbitrary"), vmem_limit_bytes=VMEM_LIMIT),
        name="proj",
    )(x, g_mix, w_in, conv_w, conv_b, gq, gk, g_out_conv, gsum)


_BRANCHES = tuple(sorted(DILATED_PAIRS, key=lambda wd: -wd[1]))
_STACK = HEADS_PER_SLAB * BLOCK
assert HEADS_PER_SLAB == 2


def _bias_tables():
    qi = np.arange(BLOCK)[:, None] + BLOCK
    kj = np.arange(2 * BLOCK)[None, :]
    step = (qi - kj).astype(np.float64)
    tabs = np.zeros((N_SLABS, len(_BRANCHES), 2, _STACK, 2 * BLOCK), np.float32)
    for bi, (window, dilation) in enumerate(_BRANCHES):
        band = (step >= 0) & (step <= window // dilation)
        for head in range(N_HEADS):
            slope = 2.0 ** (-8.0 * (head + 1) / N_HEADS)
            slab, h = divmod(head, HEADS_PER_SLAB)
            for variant, keep in enumerate((band, band & (kj >= BLOCK))):
                tabs[slab, bi, variant, h * BLOCK:(h + 1) * BLOCK] = np.where(
                    keep, -slope * step * dilation * LOG2E, NEG)
    return tabs


def _attn_kernel(q_ref, k_ref, v_ref, bias_ref, o_ref, acc_ref, m_ref):
    base = pl.program_id(2) * ATTN_TILE
    lane = lax.broadcasted_iota(jnp.int32, (BLOCK, LANES), 1)
    head_lanes = [(lane >= h * HEAD_DIM) & (lane < (h + 1) * HEAD_DIM) for h in range(HEADS_PER_SLAB)]
    heads = range(HEADS_PER_SLAB)

    def rows(start, d):
        return pl.ds(start, BLOCK, stride=d) if d > 1 else pl.ds(start, BLOCK)

    def load_keys(start, d):
        r = rows(start, d)
        v = v_ref[0, r, :]
        vext = [jnp.where(head_lanes[h], v, 1.0).astype(_BF16) for h in heads]
        return k_ref[0, r, :].astype(_BF16), vext

    def split(a):
        return [a[h * BLOCK:(h + 1) * BLOCK] for h in heads]

    def block(qs, d, bi, prev, cur, variant, init, final):
        r = rows(qs, d)
        q2 = q_ref[0, r, :]
        qq = jnp.concatenate([jnp.where(head_lanes[h], q2, 0.0) for h in heads], axis=0).astype(_BF16)
        kk = jnp.concatenate([prev[0], cur[0]], axis=0)
        s = lax.dot_general(qq, kk, (((1,), (1,)), ((), ())), preferred_element_type=_F32)
        s = s + bias_ref[0, bi, variant]
        m_new = jnp.broadcast_to(jnp.max(s, axis=-1, keepdims=True), (_STACK, LANES))
        if not init:
            m_old = jnp.concatenate([m_ref[h, r, :] for h in heads], axis=0)
            m_new = jnp.maximum(m_old, m_new)
            alpha = jnp.exp2(m_old - m_new)
        p = split(jnp.exp2((s - jnp.concatenate([m_new, m_new], axis=1)).astype(_BF16)))
        acc = jnp.concatenate(
            [_mm(p[h], jnp.concatenate([prev[1][h], cur[1][h]], axis=0)) for h in heads], axis=0)
        if not init:
            acc = alpha * jnp.concatenate([acc_ref[h, r, :] for h in heads], axis=0) + acc
        if final:
            a0, a1 = split(acc)
            num = jnp.where(head_lanes[0], a0, a1)
            den = pltpu.roll(jnp.where(head_lanes[0], a1, a0), HEAD_DIM, axis=1)
            o_ref[0, r, :] = num / den
        else:
            for h, (ah, mh) in enumerate(zip(split(acc), split(m_new))):
                acc_ref[h, r, :] = ah
                m_ref[h, r, :] = mh

    n_blocks = ATTN_TILE // BLOCK
    for bi, (_, d) in enumerate(_BRANCHES):
        run = BLOCK * d
        per_res = ATTN_TILE // run
        consec = min(ATTN_GROUP, per_res)
        n_res = ATTN_GROUP // consec
        chunks = per_res // consec
        assert chunks == 1 or d == n_res == 1

        def body(i, carry, bi=bi, d=d, run=run, consec=consec, n_res=n_res, chunks=chunks):
            res0, chunk = (i, 0) if chunks == 1 else (0, i)
            for u in range(n_res):
                qs0 = res0 * n_res + u + chunk * (consec * run)
                gs0 = base + qs0
                no_prev = gs0 < run
                keys = [load_keys(jnp.where(no_prev, gs0, gs0 - run), d)]
                keys += [load_keys(gs0 + g * run, d) for g in range(consec)]
                for g in range(consec):
                    variant = no_prev.astype(jnp.int32) if g == 0 else 0
                    block(qs0 + g * run, d, bi, keys[g], keys[g + 1], variant,
                          init=(bi == 0), final=(bi == len(_BRANCHES) - 1))
            return carry

        for i in range(n_blocks // ATTN_GROUP):
            body(i, 0)


def _attn_call(q, k, v, bias):
    b, t, _ = q.shape
    return pl.pallas_call(
        _attn_kernel,
        grid=(b, N_SLABS, t // ATTN_TILE),
        in_specs=[pl.BlockSpec((1, ATTN_TILE, LANES), lambda i, j, s: (i, s, j)),
                  pl.BlockSpec((1, t, LANES), lambda i, j, s: (i, 0, j)),
                  pl.BlockSpec((1, t, LANES), lambda i, j, s: (i, 0, j)),
                  pl.BlockSpec((1,) + bias.shape[1:], lambda i, j, s: (j, 0, 0, 0, 0))],
        out_specs=pl.BlockSpec((1, ATTN_TILE, LANES), lambda i, j, s: (i, s, j)),
        out_shape=jax.ShapeDtypeStruct((b, t, ATTN_W), _F32),
        scratch_shapes=[pltpu.VMEM((HEADS_PER_SLAB, ATTN_TILE, LANES), _F32),
                        pltpu.VMEM((HEADS_PER_SLAB, ATTN_TILE, LANES), _F32)],
        compiler_params=pltpu.CompilerParams(
            dimension_semantics=("arbitrary", "arbitrary", "arbitrary"), vmem_limit_bytes=VMEM_LIMIT),
        name="attn",
    )(q, k, v, bias)


def _ffn_kernel(x_ref, yc_ref, ya_ref, p_ref, goa_ref, wout_ref, gffn_ref, wgu_ref, fcw_ref, fcb_ref,
                wd_ref, gple_ref, wpg_ref, wpp_ref, o_ref, x1_ref, act_ref, carry_ref, work_ref):
    @pl.when(pl.program_id(1) == 0)
    def _():
        carry_ref[...] = jnp.zeros_like(carry_ref)

    ya = _rms(ya_ref[0], goa_ref[...]).astype(_BF16)
    x1 = x_ref[0] + _mm(yc_ref[0], wout_ref[0:CONV_W, :]) + _mm(ya, wout_ref[CONV_W:CONV_W + ATTN_W, :])
    x1_ref[...] = x1
    h = _rms(x1, gffn_ref[...]).astype(_BF16)

    gu_next = _mm(h, wgu_ref[0])
    for c in range(N_FF_CHUNKS):
        gu = gu_next
        if c + 1 < N_FF_CHUNKS:
            gu_next = _mm(h, wgu_ref[c + 1])
        gate = _causal_conv3(gu[:, :FF_CHUNK], fcw_ref.at[c], fcb_ref.at[c], carry_ref.at[c], work_ref.at[c % 2])
        act = gate / (1.0 + jnp.exp(-gate)) * gu[:, FF_CHUNK:]
        act_ref[:, c * FF_CHUNK:(c + 1) * FF_CHUNK] = act.astype(_BF16)

    x2 = x1_ref[...] + _mm(act_ref[...], wd_ref[...])
    h = _rms(x2, gple_ref[...]).astype(_BF16)
    gate = 1.0 / (1.0 + jnp.exp(-_mm(h, wpg_ref[...])))
    o_ref[0] = x2 + gate * _mm(p_ref[0].astype(_BF16), wpp_ref[...])


def _ffn_call(x, yc, ya, p, g_out_attn, w_out, g_ffn, wgu, fcw, fcb, wd, g_ple, wpg, wpp):
    b, t, _ = x.shape

    def const(*shape):
        return pl.BlockSpec(shape, lambda i, j: (0,) * len(shape), pipeline_mode=pl.Buffered(1))

    tile = lambda w: pl.BlockSpec((1, TOK_TILE, w), lambda i, j: (i, j, 0))
    return pl.pallas_call(
        _ffn_kernel,
        grid=(b, t // TOK_TILE),
        in_specs=[tile(D_MODEL), tile(CONV_W), tile(ATTN_W), tile(PLE_DIM),
                  const(1, ATTN_W), const(CONV_W + ATTN_W, D_MODEL), const(1, D_MODEL),
                  const(N_FF_CHUNKS, D_MODEL, 2 * FF_CHUNK),
                  const(N_FF_CHUNKS, CONV_K, FF_CHUNK), const(N_FF_CHUNKS, 1, FF_CHUNK),
                  const(D_FF, D_MODEL), const(1, D_MODEL),
                  const(D_MODEL, D_MODEL), const(PLE_DIM, D_MODEL)],
        out_specs=tile(D_MODEL),
        out_shape=jax.ShapeDtypeStruct((b, t, D_MODEL), _F32),
        scratch_shapes=[pltpu.VMEM((TOK_TILE, D_MODEL), _F32),
                        pltpu.VMEM((TOK_TILE, D_FF), _BF16),
                        pltpu.VMEM((N_FF_CHUNKS, SUBLANES, FF_CHUNK), _F32),
                        pltpu.VMEM((2, SUBLANES + TOK_TILE, FF_CHUNK), _F32)],
        compiler_params=pltpu.CompilerParams(
            dimension_semantics=("arbitrary", "arbitrary"), vmem_limit_bytes=VMEM_LIMIT),
        name="ffn",
    )(x, yc, ya, p, g_out_attn, w_out, g_ffn, wgu, fcw, fcb, wd, g_ple, wpg, wpp)


def _chunk_cols(w):
    return w.reshape(w.shape[0], N_FF_CHUNKS, FF_CHUNK).transpose(1, 0, 2)


def kernel(x, p, g_mix, w_in, conv_w, conv_b, q_norm_g, k_norm_g, g_out_conv, g_out_attn, w_out, g_ffn,
           w_gate, w_up, ffn_conv_w, ffn_conv_b, w_down, g_ple, w_ple_gate, w_ple_proj):
    depth = w_in.shape[0]
    row = lambda a: a.reshape(1, -1)
    gsum = jnp.asarray(np.kron(np.eye(N_HEADS), np.full((HEAD_DIM, HEAD_DIM), 1.0 / HEAD_DIM)), _BF16)
    bias = jnp.asarray(_bias_tables())
    for i in range(depth):
        yc, q, k, v = _proj_call(
            x, row(g_mix[i]), w_in[i].astype(_BF16), conv_w[i], row(conv_b[i]),
            row(jnp.tile(q_norm_g[i], N_HEADS)), row(jnp.tile(k_norm_g[i], N_HEADS)),
            row(g_out_conv[i]), gsum)
        ya = _attn_call(q, k, v, bias)
        wgu = jnp.concatenate([_chunk_cols(w_gate[i]), _chunk_cols(w_up[i])], axis=-1).astype(_BF16)
        x = _ffn_call(
            x, yc, ya, p[i], row(g_out_attn[i]), w_out[i].astype(_BF16), row(g_ffn[i]), wgu,
            _chunk_cols(ffn_conv_w[i]), _chunk_cols(row(ffn_conv_b[i])),
            w_down[i].astype(_BF16), row(g_ple[i]),
            w_ple_gate[i].astype(_BF16), w_ple_proj[i].astype(_BF16))
    return x
```

```python
import numpy as np
import jax
import jax.numpy as jnp
from jax import lax
from jax.experimental import pallas as pl
from jax.experimental.pallas import tpu as pltpu

D_MODEL = 1024
CONV_W = 512
N_HEADS = 8
HEAD_DIM = 64
ATTN_W = N_HEADS * HEAD_DIM
CONV_K = 3
D_FF = 2816
PLE_DIM = 256
DILATED_PAIRS = ((128, 1), (512, 4), (2048, 16))
BLOCK = 128
EPS = 1e-6

LANES = 128
SUBLANES = 8
HEADS_PER_SLAB = LANES // HEAD_DIM
N_SLABS = ATTN_W // LANES

TOK_TILE = 512
ATTN_TILE = 2048
FF_CHUNK = 256
N_FF_CHUNKS = D_FF // FF_CHUNK
NEG = -1e30
LOG2E = 1.4426950408889634
VMEM_LIMIT = 56 * 1024 * 1024

_F32 = jnp.float32
_BF16 = jnp.bfloat16


def _rms(x, g):
    ms = jnp.mean(x * x, axis=-1, keepdims=True)
    return x * lax.rsqrt(ms + EPS) * g


def _mm(a, b):
    return jnp.dot(a, b, preferred_element_type=_F32)


def _causal_conv3(u, w_ref, b_ref, carry_ref, work_ref):
    n = u.shape[0]
    work_ref[0:SUBLANES, :] = carry_ref[...]
    work_ref[SUBLANES:SUBLANES + n, :] = u
    u1 = work_ref[SUBLANES - 1:SUBLANES - 1 + n, :]
    u2 = work_ref[SUBLANES - 2:SUBLANES - 2 + n, :]
    carry_ref[...] = work_ref[n:n + SUBLANES, :]
    return u2 * w_ref[0:1, :] + u1 * w_ref[1:2, :] + u * w_ref[2:3, :] + b_ref[...]


def _proj_kernel(x_ref, gmix_ref, win_ref, cw_ref, cb_ref, gq_ref, gk_ref, goc_ref,
                 yc_ref, q_ref, k_ref, v_ref, carry_ref, work_ref):
    @pl.when(pl.program_id(1) == 0)
    def _():
        carry_ref[...] = jnp.zeros_like(carry_ref)

    h = _rms(x_ref[0], gmix_ref[...]).astype(_BF16)

    def proj(c):
        return _mm(h, win_ref[:, c * CONV_W:(c + 1) * CONV_W])

    u = proj(1) * proj(2)
    conv = _causal_conv3(u, cw_ref, cb_ref, carry_ref, work_ref)
    yc = proj(0) * conv
    yc_ref[0] = _rms(yc, goc_ref[...]).astype(_BF16)

    lane = lax.broadcasted_iota(jnp.int32, (TOK_TILE, LANES), 1)

    def head_norm(z, g_ref):
        zz = z * z
        slabs = []
        for c in range(N_SLABS):
            blk = zz[:, c * LANES:(c + 1) * LANES]
            ms = jnp.zeros_like(blk)
            for h in range(HEADS_PER_SLAB):
                own = (lane >= h * HEAD_DIM) & (lane < (h + 1) * HEAD_DIM)
                ms = jnp.where(own, jnp.sum(jnp.where(own, blk, 0.0), axis=-1, keepdims=True), ms)
            slabs.append(ms * (1.0 / HEAD_DIM))
        ms = jnp.concatenate(slabs, axis=-1)
        return z * lax.rsqrt(ms + EPS) * g_ref[...]

    q_ref[0] = head_norm(proj(3), gq_ref) * (HEAD_DIM ** -0.5 * LOG2E)
    k_ref[0] = head_norm(proj(4), gk_ref)
    v_ref[0] = proj(5)


def _proj_call(x, g_mix, w_in, conv_w, conv_b, gq, gk, g_out_conv):
    b, t, _ = x.shape
    const = lambda *shape: pl.BlockSpec(shape, lambda i, j: (0,) * len(shape))
    tile = lambda w: pl.BlockSpec((1, TOK_TILE, w), lambda i, j: (i, j, 0))
    return pl.pallas_call(
        _proj_kernel,
        grid=(b, t // TOK_TILE),
        in_specs=[tile(D_MODEL), const(1, D_MODEL), const(D_MODEL, 3 * CONV_W + 3 * ATTN_W),
                  const(CONV_K, CONV_W), const(1, CONV_W), const(1, ATTN_W), const(1, ATTN_W),
                  const(1, CONV_W)],
        out_specs=[tile(CONV_W), tile(ATTN_W), tile(ATTN_W), tile(ATTN_W)],
        out_shape=[jax.ShapeDtypeStruct((b, t, CONV_W), _BF16),
                   jax.ShapeDtypeStruct((b, t, ATTN_W), _F32),
                   jax.ShapeDtypeStruct((b, t, ATTN_W), _F32),
                   jax.ShapeDtypeStruct((b, t, ATTN_W), _F32)],
        scratch_shapes=[pltpu.VMEM((SUBLANES, CONV_W), _F32),
                        pltpu.VMEM((SUBLANES + TOK_TILE, CONV_W), _F32)],
        compiler_params=pltpu.CompilerParams(
            dimension_semantics=("arbitrary", "arbitrary"), vmem_limit_bytes=VMEM_LIMIT),
        name="proj",
    )(x, g_mix, w_in, conv_w, conv_b, gq, gk, g_out_conv)


_BRANCHES = tuple(sorted(DILATED_PAIRS, key=lambda wd: -wd[1]))
_STACK = HEADS_PER_SLAB * BLOCK
assert HEADS_PER_SLAB == 2


_DILATIONS = tuple(d for _, d in _BRANCHES)
assert _DILATIONS == (16, 4, 1)
N_PLANES = _DILATIONS[1]
PLANE_ROWS = ATTN_TILE // N_PLANES
PLANE_STRIDE = _DILATIONS[0] // N_PLANES
PLANE_BLOCKS = PLANE_ROWS // BLOCK
TILE_BLOCKS = ATTN_TILE // BLOCK
RUN = BLOCK // N_PLANES
_PLANE_ORDER = np.array([N_PLANES * (p % RUN) + p // RUN for p in range(BLOCK)])


def _bias_tables():
    qi = np.arange(BLOCK)[:, None] + BLOCK
    kj = np.arange(2 * BLOCK)[None, :]
    step = (qi - kj).astype(np.float64)
    tabs = np.zeros((N_SLABS, len(_BRANCHES), 2, _STACK, 2 * BLOCK), np.float32)
    for bi, (window, dilation) in enumerate(_BRANCHES):
        band = (step >= 0) & (step <= window // dilation)
        for head in range(N_HEADS):
            slope = 2.0 ** (-8.0 * (head + 1) / N_HEADS)
            slab, h = divmod(head, HEADS_PER_SLAB)
            for variant, keep in enumerate((band, band & (kj >= BLOCK))):
                tab = np.where(keep, -slope * step * dilation * LOG2E, NEG)
                if dilation == 1:
                    tab = tab[_PLANE_ORDER][:, np.concatenate([_PLANE_ORDER, BLOCK + _PLANE_ORDER])]
                tabs[slab, bi, variant, h * BLOCK:(h + 1) * BLOCK] = tab
    return tabs


def _attn_kernel(q_ref, k_ref, v_ref, bias_ref, o_ref, qpl, kpl, vpl, acc_pl, m_pl):
    tile = pl.program_id(2)
    first_tile = tile == 0
    slot = tile % 2
    prev_slot = jnp.where(first_tile, slot, 1 - slot)
    no_prev = first_tile.astype(jnp.int32)
    heads = range(HEADS_PER_SLAB)

    def head_lanes(n_rows):
        lane = lax.broadcasted_iota(jnp.int32, (n_rows, LANES), 1)
        return [(lane >= h * HEAD_DIM) & (lane < (h + 1) * HEAD_DIM) for h in heads]

    def stack(parts):
        return jnp.concatenate(parts, axis=0)

    def split(a):
        return [a[h * BLOCK:(h + 1) * BLOCK] for h in heads]

    def expand_q(q, masks):
        return [jnp.where(masks[h], q, 0.0) for h in heads]

    def expand_v(v, masks):
        return [jnp.where(masks[h], v, 1.0) for h in heads]

    plane_masks = head_lanes(PLANE_ROWS)
    for r in range(N_PLANES):
        plane = pl.ds(r, PLANE_ROWS, stride=N_PLANES)
        qe = expand_q(q_ref[0, plane, :], plane_masks)
        ve = expand_v(v_ref[0, plane, :], plane_masks)
        kpl[slot, r] = k_ref[0, plane, :]
        for h in heads:
            qpl[h, r] = qe[h]
            vpl[slot, h, r] = ve[h]

    def attend(bi, variant, qq, keys, state):
        (k0, v0), (k1, v1) = keys
        s = lax.dot_general(qq, stack([k0, k1]), (((1,), (1,)), ((), ())), preferred_element_type=_F32)
        s = s + bias_ref[0, bi, variant]
        m_new = jnp.broadcast_to(jnp.max(s, axis=-1, keepdims=True), (_STACK, LANES))
        if state is not None:
            m_old, acc_old = state
            m_new = jnp.maximum(m_old, m_new)
            alpha = jnp.exp2(m_old - m_new)
        p = split(jnp.exp2((s - jnp.concatenate([m_new, m_new], axis=1)).astype(_BF16)))
        acc = stack([_mm(p[h], stack([v0[h], v1[h]])) for h in heads])
        if state is not None:
            acc = alpha * acc_old + acc
        return m_new, acc

    def plane_keys(sl, r, rows):
        return kpl[sl, r, rows, :].astype(_BF16), [vpl[sl, h, r, rows, :].astype(_BF16) for h in heads]

    def plane_q(r, rows):
        return stack([qpl[h, r, rows, :] for h in heads]).astype(_BF16)

    def load_state(r, rows):
        return stack([m_pl[h, r, rows, :] for h in heads]), stack([acc_pl[h, r, rows, :] for h in heads])

    def store_state(r, rows, m, acc):
        for h, (ah, mh) in enumerate(zip(split(acc), split(m))):
            acc_pl[h, r, rows, :] = ah
            m_pl[h, r, rows, :] = mh

    for r in range(N_PLANES):
        for a in range(PLANE_STRIDE):
            rows = pl.ds(a, BLOCK, stride=PLANE_STRIDE)
            keys = (plane_keys(prev_slot, r, rows), plane_keys(slot, r, rows))
            store_state(r, rows, *attend(0, no_prev, plane_q(r, rows), keys, None))
    def gather(load, j):
        return stack([load(r, pl.ds(j * RUN, RUN)) for r in range(N_PLANES)])

    def token_keys(sl, j):
        k = gather(lambda r, rows: kpl[sl, r, rows, :], j).astype(_BF16)
        return k, [gather(lambda r, rows, h=h: vpl[sl, h, r, rows, :], j).astype(_BF16) for h in heads]

    masks = head_lanes(BLOCK)
    tkeys = [token_keys(prev_slot, TILE_BLOCKS - 1)]

    def token_block(j):
        tkeys.append(token_keys(slot, j))
        qq = stack([gather(lambda r, rows, h=h: qpl[h, r, rows, :], j) for h in heads]).astype(_BF16)
        state = (stack([gather(lambda r, rows, h=h: m_pl[h, r, rows, :], j) for h in heads]),
                 stack([gather(lambda r, rows, h=h: acc_pl[h, r, rows, :], j) for h in heads]))
        _, acc = attend(2, no_prev if j == 0 else 0, qq, (tkeys[j], tkeys[j + 1]), state)
        a0, a1 = split(acc)
        num = jnp.where(masks[0], a0, a1)
        den = pltpu.roll(jnp.where(masks[0], a1, a0), HEAD_DIM, axis=1)
        out = num / den
        for r in range(N_PLANES):
            o_ref[0, pl.ds(j * BLOCK + r, RUN, stride=N_PLANES), :] = out[r * RUN:(r + 1) * RUN]

    keys = [[plane_keys(prev_slot, r, pl.ds(PLANE_ROWS - BLOCK, BLOCK))] for r in range(N_PLANES)]
    for j in range(PLANE_BLOCKS):
        rows = pl.ds(j * BLOCK, BLOCK)
        for r in range(N_PLANES):
            keys[r].append(plane_keys(slot, r, rows))
            store_state(r, rows, *attend(1, no_prev if j == 0 else 0, plane_q(r, rows),
                                         (keys[r][j], keys[r][j + 1]), load_state(r, rows)))
        for jj in range(N_PLANES):
            token_block(j * N_PLANES + jj)


def _attn_call(q, k, v, bias):
    b, t, _ = q.shape
    planes = (N_PLANES, PLANE_ROWS, LANES)
    tile = pl.BlockSpec((1, ATTN_TILE, LANES), lambda i, j, s: (i, s, j))
    return pl.pallas_call(
        _attn_kernel,
        grid=(b, N_SLABS, t // ATTN_TILE),
        in_specs=[tile, tile, tile,
                  pl.BlockSpec((1,) + bias.shape[1:], lambda i, j, s: (j, 0, 0, 0, 0))],
        out_specs=tile,
        out_shape=jax.ShapeDtypeStruct((b, t, ATTN_W), _F32),
        scratch_shapes=[pltpu.VMEM((HEADS_PER_SLAB,) + planes, _F32),
                        pltpu.VMEM((2,) + planes, _F32),
                        pltpu.VMEM((2, HEADS_PER_SLAB) + planes, _F32),
                        pltpu.VMEM((HEADS_PER_SLAB,) + planes, _F32),
                        pltpu.VMEM((HEADS_PER_SLAB,) + planes, _F32)],
        compiler_params=pltpu.CompilerParams(
            dimension_semantics=("arbitrary", "arbitrary", "arbitrary"), vmem_limit_bytes=VMEM_LIMIT),
        name="attn",
    )(q, k, v, bias)


def _ffn_kernel(x_ref, yc_ref, ya_ref, p_ref, goa_ref, wout_ref, gffn_ref, wgu_ref, fcw_ref, fcb_ref,
                wd_ref, gple_ref, wpg_ref, wpp_ref, o_ref, x1_ref, act_ref, carry_ref, work_ref):
    @pl.when(pl.program_id(1) == 0)
    def _():
        carry_ref[...] = jnp.zeros_like(carry_ref)

    ya = _rms(ya_ref[0], goa_ref[...]).astype(_BF16)
    x1 = x_ref[0] + _mm(yc_ref[0], wout_ref[0:CONV_W, :]) + _mm(ya, wout_ref[CONV_W:CONV_W + ATTN_W, :])
    x1_ref[...] = x1
    h = _rms(x1, gffn_ref[...]).astype(_BF16)

    gu_next = _mm(h, wgu_ref[0])
    for c in range(N_FF_CHUNKS):
        gu = gu_next
        if c + 1 < N_FF_CHUNKS:
            gu_next = _mm(h, wgu_ref[c + 1])
        gate = _causal_conv3(gu[:, :FF_CHUNK], fcw_ref.at[c], fcb_ref.at[c], carry_ref.at[c], work_ref.at[c % 2])
        act = gate / (1.0 + jnp.exp(-gate)) * gu[:, FF_CHUNK:]
        act_ref[:, c * FF_CHUNK:(c + 1) * FF_CHUNK] = act.astype(_BF16)

    x2 = x1_ref[...] + _mm(act_ref[...], wd_ref[...])
    h = _rms(x2, gple_ref[...]).astype(_BF16)
    gate = 1.0 / (1.0 + jnp.exp(-_mm(h, wpg_ref[...])))
    o_ref[0] = x2 + gate * _mm(p_ref[0].astype(_BF16), wpp_ref[...])


def _ffn_call(x, yc, ya, p, g_out_attn, w_out, g_ffn, wgu, fcw, fcb, wd, g_ple, wpg, wpp):
    b, t, _ = x.shape

    def const(*shape):
        return pl.BlockSpec(shape, lambda i, j: (0,) * len(shape), pipeline_mode=pl.Buffered(1))

    tile = lambda w: pl.BlockSpec((1, TOK_TILE, w), lambda i, j: (i, j, 0))
    return pl.pallas_call(
        _ffn_kernel,
        grid=(b, t // TOK_TILE),
        in_specs=[tile(D_MODEL), tile(CONV_W), tile(ATTN_W), tile(PLE_DIM),
                  const(1, ATTN_W), const(CONV_W + ATTN_W, D_MODEL), const(1, D_MODEL),
                  const(N_FF_CHUNKS, D_MODEL, 2 * FF_CHUNK),
                  const(N_FF_CHUNKS, CONV_K, FF_CHUNK), const(N_FF_CHUNKS, 1, FF_CHUNK),
                  const(D_FF, D_MODEL), const(1, D_MODEL),
                  const(D_MODEL, D_MODEL), const(PLE_DIM, D_MODEL)],
        out_specs=tile(D_MODEL),
        out_shape=jax.ShapeDtypeStruct((b, t, D_MODEL), _F32),
        scratch_shapes=[pltpu.VMEM((TOK_TILE, D_MODEL), _F32),
                        pltpu.VMEM((TOK_TILE, D_FF), _BF16),
                        pltpu.VMEM((N_FF_CHUNKS, SUBLANES, FF_CHUNK), _F32),
                        pltpu.VMEM((2, SUBLANES + TOK_TILE, FF_CHUNK), _F32)],
        compiler_params=pltpu.CompilerParams(
            dimension_semantics=("arbitrary", "arbitrary"), vmem_limit_bytes=VMEM_LIMIT),
        name="ffn",
    )(x, yc, ya, p, g_out_attn, w_out, g_ffn, wgu, fcw, fcb, wd, g_ple, wpg, wpp)


def _chunk_cols(w):
    return w.reshape(w.shape[0], N_FF_CHUNKS, FF_CHUNK).transpose(1, 0, 2)


def kernel(x, p, g_mix, w_in, conv_w, conv_b, q_norm_g, k_norm_g, g_out_conv, g_out_attn, w_out, g_ffn,
           w_gate, w_up, ffn_conv_w, ffn_conv_b, w_down, g_ple, w_ple_gate, w_ple_proj):
    depth = w_in.shape[0]
    row = lambda a: a.reshape(1, -1)
    bias = jnp.asarray(_bias_tables())
    for i in range(depth):
        yc, q, k, v = _proj_call(
            x, row(g_mix[i]), w_in[i].astype(_BF16), conv_w[i], row(conv_b[i]),
            row(jnp.tile(q_norm_g[i], N_HEADS)), row(jnp.tile(k_norm_g[i], N_HEADS)),
            row(g_out_conv[i]))
        ya = _attn_call(q, k, v, bias)
        wgu = jnp.concatenate([_chunk_cols(w_gate[i]), _chunk_cols(w_up[i])], axis=-1).astype(_BF16)
        x = _ffn_call(
            x, yc, ya, p[i], row(g_out_attn[i]), w_out[i].astype(_BF16), row(g_ffn[i]), wgu,
            _chunk_cols(ffn_conv_w[i]), _chunk_cols(row(ffn_conv_b[i])),
            w_down[i].astype(_BF16), row(g_ple[i]),
            w_ple_gate[i].astype(_BF16), w_ple_proj[i].astype(_BF16))
    return x
```

```python
import numpy as np
import jax
import jax.numpy as jnp
from jax import lax
from jax.experimental import pallas as pl
from jax.experimental.pallas import tpu as pltpu

D_MODEL = 1024
CONV_W = 512
N_HEADS = 8
HEAD_DIM = 64
ATTN_W = N_HEADS * HEAD_DIM
CONV_K = 3
D_FF = 2816
PLE_DIM = 256
DILATED_PAIRS = ((128, 1), (512, 4), (2048, 16))
BLOCK = 128
EPS = 1e-6

LANES = 128
SUBLANES = 8
HEADS_PER_SLAB = LANES // HEAD_DIM
N_SLABS = ATTN_W // LANES

PROJ_TILE = 1024
TOK_TILE = 512
ATTN_TILE = 2048
FF_CHUNK = 256
N_FF_CHUNKS = D_FF // FF_CHUNK
NEG = -1e30
LOG2E = 1.4426950408889634
VMEM_LIMIT = 56 * 1024 * 1024

_F32 = jnp.float32
_BF16 = jnp.bfloat16


def _rms(x, g):
    ms = jnp.mean(x * x, axis=-1, keepdims=True)
    return x * lax.rsqrt(ms + EPS) * g


def _mm(a, b):
    return jnp.dot(a, b, preferred_element_type=_F32)


def _causal_conv3(u, w, b, carry_ref, work_ref):
    n = u.shape[0]
    work_ref[0:SUBLANES, :] = carry_ref[...]
    work_ref[SUBLANES:SUBLANES + n, :] = u
    u1 = work_ref[SUBLANES - 1:SUBLANES - 1 + n, :]
    u2 = work_ref[SUBLANES - 2:SUBLANES - 2 + n, :]
    carry_ref[...] = work_ref[n:n + SUBLANES, :]
    return u2 * w[0:1, :] + u1 * w[1:2, :] + u * w[2:3, :] + b


def _proj_kernel(x_ref, gmix_ref, win_ref, cw_ref, cb_ref, gq_ref, gk_ref, goc_ref,
                 yc_ref, q_ref, k_ref, v_ref, carry_ref, work_ref):
    @pl.when(pl.program_id(1) == 0)
    def _():
        carry_ref[...] = jnp.zeros_like(carry_ref)

    h = _rms(x_ref[0], gmix_ref[...]).astype(_BF16)

    def proj(c):
        return _mm(h, win_ref[:, c * CONV_W:(c + 1) * CONV_W])

    u = proj(1) * proj(2)
    conv = _causal_conv3(u, cw_ref[...], cb_ref[...], carry_ref, work_ref)
    yc = proj(0) * conv
    yc_ref[0] = _rms(yc, goc_ref[...]).astype(_BF16)

    lane = lax.broadcasted_iota(jnp.int32, (PROJ_TILE, LANES), 1)

    def head_norm(z, g_ref):
        zz = z * z
        slabs = []
        for c in range(N_SLABS):
            blk = zz[:, c * LANES:(c + 1) * LANES]
            ms = jnp.zeros_like(blk)
            for h in range(HEADS_PER_SLAB):
                own = (lane >= h * HEAD_DIM) & (lane < (h + 1) * HEAD_DIM)
                ms = jnp.where(own, jnp.sum(jnp.where(own, blk, 0.0), axis=-1, keepdims=True), ms)
            slabs.append(ms * (1.0 / HEAD_DIM))
        ms = jnp.concatenate(slabs, axis=-1)
        return z * lax.rsqrt(ms + EPS) * g_ref[...]

    q_ref[0] = head_norm(proj(3), gq_ref) * (HEAD_DIM ** -0.5 * LOG2E)
    k_ref[0] = head_norm(proj(4), gk_ref)
    v_ref[0] = proj(5)


def _proj_call(x, g_mix, w_in, conv_w, conv_b, gq, gk, g_out_conv):
    b, t, _ = x.shape
    const = lambda *shape: pl.BlockSpec(shape, lambda i, j: (0,) * len(shape), pipeline_mode=pl.Buffered(1))
    tile = lambda w: pl.BlockSpec((1, PROJ_TILE, w), lambda i, j: (i, j, 0))
    return pl.pallas_call(
        _proj_kernel,
        grid=(b, t // PROJ_TILE),
        in_specs=[tile(D_MODEL), const(1, D_MODEL), const(D_MODEL, 3 * CONV_W + 3 * ATTN_W),
                  const(CONV_K, CONV_W), const(1, CONV_W), const(1, ATTN_W), const(1, ATTN_W),
                  const(1, CONV_W)],
        out_specs=[tile(CONV_W), tile(ATTN_W), tile(ATTN_W), tile(ATTN_W)],
        out_shape=[jax.ShapeDtypeStruct((b, t, CONV_W), _BF16),
                   jax.ShapeDtypeStruct((b, t, ATTN_W), _F32),
                   jax.ShapeDtypeStruct((b, t, ATTN_W), _F32),
                   jax.ShapeDtypeStruct((b, t, ATTN_W), _F32)],
        scratch_shapes=[pltpu.VMEM((SUBLANES, CONV_W), _F32),
                        pltpu.VMEM((SUBLANES + PROJ_TILE, CONV_W), _F32)],
        compiler_params=pltpu.CompilerParams(
            dimension_semantics=("arbitrary", "arbitrary"), vmem_limit_bytes=VMEM_LIMIT),
        name="proj",
    )(x, g_mix, w_in, conv_w, conv_b, gq, gk, g_out_conv)


_BRANCHES = tuple(sorted(DILATED_PAIRS, key=lambda wd: -wd[1]))
_STACK = HEADS_PER_SLAB * BLOCK
assert HEADS_PER_SLAB == 2


_DILATIONS = tuple(d for _, d in _BRANCHES)
assert _DILATIONS == (16, 4, 1)
N_PLANES = _DILATIONS[1]
PLANE_ROWS = ATTN_TILE // N_PLANES
PLANE_STRIDE = _DILATIONS[0] // N_PLANES
PLANE_BLOCKS = PLANE_ROWS // BLOCK
SUB_BLOCKS = PLANE_BLOCKS // PLANE_STRIDE
TILE_BLOCKS = ATTN_TILE // BLOCK
RUN = BLOCK // N_PLANES
_PLANE_ORDER = np.array([N_PLANES * (p % RUN) + p // RUN for p in range(BLOCK)])


def _bias_tables():
    qi = np.arange(BLOCK)[:, None] + BLOCK
    kj = np.arange(2 * BLOCK)[None, :]
    step = (qi - kj).astype(np.float64)
    tabs = np.zeros((N_SLABS, len(_BRANCHES), 2, _STACK, 2 * BLOCK), np.float32)
    for bi, (window, dilation) in enumerate(_BRANCHES):
        band = (step >= 0) & (step <= window // dilation)
        for head in range(N_HEADS):
            slope = 2.0 ** (-8.0 * (head + 1) / N_HEADS)
            slab, h = divmod(head, HEADS_PER_SLAB)
            for variant, keep in enumerate((band, band & (kj >= BLOCK))):
                tab = np.where(keep, -slope * step * dilation * LOG2E, NEG)
                if dilation == 1:
                    tab = tab[_PLANE_ORDER][:, np.concatenate([_PLANE_ORDER, BLOCK + _PLANE_ORDER])]
                tabs[slab, bi, variant, h * BLOCK:(h + 1) * BLOCK] = tab
    return tabs


def _attn_kernel(q_ref, k_ref, v_ref, bias_ref, o_ref, qpl, kpl, vpl, acc_pl, m_pl):
    tile = pl.program_id(2)
    first_tile = tile == 0
    slot = tile % 2
    prev_slot = jnp.where(first_tile, slot, 1 - slot)
    no_prev = first_tile.astype(jnp.int32)
    heads = range(HEADS_PER_SLAB)

    def head_lanes(n_rows):
        lane = lax.broadcasted_iota(jnp.int32, (n_rows, LANES), 1)
        return [(lane >= h * HEAD_DIM) & (lane < (h + 1) * HEAD_DIM) for h in heads]

    def stack(parts):
        return jnp.concatenate(parts, axis=0)

    def split(a):
        return [a[h * BLOCK:(h + 1) * BLOCK] for h in heads]

    def expand_q(q, masks):
        return [jnp.where(masks[h], q, 0.0) for h in heads]

    def expand_v(v, masks):
        return [jnp.where(masks[h], v, 1.0) for h in heads]

    plane_masks = head_lanes(PLANE_ROWS)
    for r in range(N_PLANES):
        plane = pl.ds(r, PLANE_ROWS, stride=N_PLANES)
        qe = expand_q(q_ref[0, plane, :], plane_masks)
        ve = expand_v(v_ref[0, plane, :], plane_masks)
        kpl[slot, r] = k_ref[0, plane, :]
        for h in heads:
            qpl[h, r] = qe[h]
            vpl[slot, h, r] = ve[h]

    def attend(bi, variant, qq, keys, state):
        (k0, v0), (k1, v1) = keys
        s = lax.dot_general(qq, stack([k0, k1]), (((1,), (1,)), ((), ())), preferred_element_type=_F32)
        s = s + bias_ref[0, bi, variant]
        m_new = jnp.broadcast_to(jnp.max(s, axis=-1, keepdims=True), (_STACK, LANES))
        if state is not None:
            m_old, acc_old = state
            m_new = jnp.maximum(m_old, m_new)
            alpha = jnp.exp2(m_old - m_new)
        p = split(jnp.exp2((s - jnp.concatenate([m_new, m_new], axis=1)).astype(_BF16)))
        acc = stack([_mm(p[h], stack([v0[h], v1[h]])) for h in heads])
        if state is not None:
            acc = alpha * acc_old + acc
        return m_new, acc

    def plane_keys(sl, r, rows):
        return kpl[sl, r, rows, :].astype(_BF16), [vpl[sl, h, r, rows, :].astype(_BF16) for h in heads]

    def plane_q(r, rows):
        return stack([qpl[h, r, rows, :] for h in heads]).astype(_BF16)

    def load_state(r, rows):
        return stack([m_pl[h, r, rows, :] for h in heads]), stack([acc_pl[h, r, rows, :] for h in heads])

    def store_state(r, rows, m, acc):
        for h, (ah, mh) in enumerate(zip(split(acc), split(m))):
            acc_pl[h, r, rows, :] = ah
            m_pl[h, r, rows, :] = mh

    for r in range(N_PLANES):
        for a in range(PLANE_STRIDE):
            sub = lambda j, a=a: pl.ds(a + j * BLOCK * PLANE_STRIDE, BLOCK, stride=PLANE_STRIDE)
            keys = [plane_keys(prev_slot, r, sub(SUB_BLOCKS - 1))]
            for j in range(SUB_BLOCKS):
                keys.append(plane_keys(slot, r, sub(j)))
                store_state(r, sub(j), *attend(0, no_prev if j == 0 else 0, plane_q(r, sub(j)),
                                               (keys[j], keys[j + 1]), None))
    def gather(load, j):
        return stack([load(r, pl.ds(j * RUN, RUN)) for r in range(N_PLANES)])

    def token_keys(sl, j):
        k = gather(lambda r, rows: kpl[sl, r, rows, :], j).astype(_BF16)
        return k, [gather(lambda r, rows, h=h: vpl[sl, h, r, rows, :], j).astype(_BF16) for h in heads]

    masks = head_lanes(BLOCK)
    tkeys = [token_keys(prev_slot, TILE_BLOCKS - 1)]

    def token_block(j):
        tkeys.append(token_keys(slot, j))
        qq = stack([gather(lambda r, rows, h=h: qpl[h, r, rows, :], j) for h in heads]).astype(_BF16)
        state = (stack([gather(lambda r, rows, h=h: m_pl[h, r, rows, :], j) for h in heads]),
                 stack([gather(lambda r, rows, h=h: acc_pl[h, r, rows, :], j) for h in heads]))
        _, acc = attend(2, no_prev if j == 0 else 0, qq, (tkeys[j], tkeys[j + 1]), state)
        a0, a1 = split(acc)
        num = jnp.where(masks[0], a0, a1)
        den = pltpu.roll(jnp.where(masks[0], a1, a0), HEAD_DIM, axis=1)
        out = num / den
        for r in range(N_PLANES):
            o_ref[0, pl.ds(j * BLOCK + r, RUN, stride=N_PLANES), :] = out[r * RUN:(r + 1) * RUN]

    keys = [[plane_keys(prev_slot, r, pl.ds(PLANE_ROWS - BLOCK, BLOCK))] for r in range(N_PLANES)]
    for j in range(PLANE_BLOCKS):
        rows = pl.ds(j * BLOCK, BLOCK)
        for r in range(N_PLANES):
            keys[r].append(plane_keys(slot, r, rows))
            store_state(r, rows, *attend(1, no_prev if j == 0 else 0, plane_q(r, rows),
                                         (keys[r][j], keys[r][j + 1]), load_state(r, rows)))
        for jj in range(N_PLANES):
            token_block(j * N_PLANES + jj)


def _attn_call(q, k, v, bias):
    b, t, _ = q.shape
    planes = (N_PLANES, PLANE_ROWS, LANES)
    tile = pl.BlockSpec((1, ATTN_TILE, LANES), lambda i, j, s: (i, s, j))
    return pl.pallas_call(
        _attn_kernel,
        grid=(b, N_SLABS, t // ATTN_TILE),
        in_specs=[tile, tile, tile,
                  pl.BlockSpec((1,) + bias.shape[1:], lambda i, j, s: (j, 0, 0, 0, 0))],
        out_specs=tile,
        out_shape=jax.ShapeDtypeStruct((b, t, ATTN_W), _F32),
        scratch_shapes=[pltpu.VMEM((HEADS_PER_SLAB,) + planes, _F32),
                        pltpu.VMEM((2,) + planes, _F32),
                        pltpu.VMEM((2, HEADS_PER_SLAB) + planes, _F32),
                        pltpu.VMEM((HEADS_PER_SLAB,) + planes, _F32),
                        pltpu.VMEM((HEADS_PER_SLAB,) + planes, _F32)],
        compiler_params=pltpu.CompilerParams(
            dimension_semantics=("arbitrary", "arbitrary", "arbitrary"), vmem_limit_bytes=VMEM_LIMIT),
        name="attn",
    )(q, k, v, bias)


def _ffn_kernel(x_ref, yc_ref, ya_ref, p_ref, goa_ref, wout_ref, gffn_ref, wg_ref, wu_ref, fcw_ref, fcb_ref,
                wd_ref, gple_ref, wpg_ref, wpp_ref, o_ref, x1_ref, act_ref, carry_ref, work_ref):
    @pl.when(pl.program_id(1) == 0)
    def _():
        carry_ref[...] = jnp.zeros_like(carry_ref)

    ya = _rms(ya_ref[0], goa_ref[...]).astype(_BF16)
    x1 = x_ref[0] + _mm(yc_ref[0], wout_ref[0:CONV_W, :]) + _mm(ya, wout_ref[CONV_W:CONV_W + ATTN_W, :])
    x1_ref[...] = x1
    h = _rms(x1, gffn_ref[...]).astype(_BF16)

    def gate_up(c):
        cols = slice(c * FF_CHUNK, (c + 1) * FF_CHUNK)
        return _mm(h, jnp.concatenate([wg_ref[:, cols], wu_ref[:, cols]], axis=1))

    gu_next = gate_up(0)
    for c in range(N_FF_CHUNKS):
        gu = gu_next
        if c + 1 < N_FF_CHUNKS:
            gu_next = gate_up(c + 1)
        cols = slice(c * FF_CHUNK, (c + 1) * FF_CHUNK)
        gate = _causal_conv3(gu[:, :FF_CHUNK], fcw_ref[:, cols], fcb_ref[:, cols], carry_ref.at[c],
                             work_ref.at[c % 2])
        act = gate / (1.0 + jnp.exp(-gate)) * gu[:, FF_CHUNK:]
        act_ref[:, cols] = act.astype(_BF16)

    x2 = x1_ref[...] + _mm(act_ref[...], wd_ref[...])
    h3 = _rms(x2, gple_ref[...]).astype(_BF16)
    gate = 1.0 / (1.0 + jnp.exp(-_mm(h3, wpg_ref[...])))
    o_ref[0] = x2 + gate * _mm(p_ref[0].astype(_BF16), wpp_ref[...])


def _ffn_call(x, yc, ya, p, g_out_attn, w_out, g_ffn, wg, wu, fcw, fcb, wd, g_ple, wpg, wpp):
    b, t, _ = x.shape

    def const(*shape):
        return pl.BlockSpec(shape, lambda i, j: (0,) * len(shape), pipeline_mode=pl.Buffered(1))

    tile = lambda w: pl.BlockSpec((1, TOK_TILE, w), lambda i, j: (i, j, 0))
    return pl.pallas_call(
        _ffn_kernel,
        grid=(b, t // TOK_TILE),
        in_specs=[tile(D_MODEL), tile(CONV_W), tile(ATTN_W), tile(PLE_DIM),
                  const(1, ATTN_W), const(CONV_W + ATTN_W, D_MODEL), const(1, D_MODEL),
                  const(D_MODEL, D_FF), const(D_MODEL, D_FF), const(CONV_K, D_FF), const(1, D_FF),
                  const(D_FF, D_MODEL), const(1, D_MODEL),
                  const(D_MODEL, D_MODEL), const(PLE_DIM, D_MODEL)],
        out_specs=tile(D_MODEL),
        out_shape=jax.ShapeDtypeStruct((b, t, D_MODEL), _F32),
        scratch_shapes=[pltpu.VMEM((TOK_TILE, D_MODEL), _F32),
                        pltpu.VMEM((TOK_TILE, D_FF), _BF16),
                        pltpu.VMEM((N_FF_CHUNKS, SUBLANES, FF_CHUNK), _F32),
                        pltpu.VMEM((2, SUBLANES + TOK_TILE, FF_CHUNK), _F32)],
        compiler_params=pltpu.CompilerParams(
            dimension_semantics=("arbitrary", "arbitrary"), vmem_limit_bytes=VMEM_LIMIT),
        name="ffn",
    )(x, yc, ya, p, g_out_attn, w_out, g_ffn, wg, wu, fcw, fcb, wd, g_ple, wpg, wpp)


def kernel(x, p, g_mix, w_in, conv_w, conv_b, q_norm_g, k_norm_g, g_out_conv, g_out_attn, w_out, g_ffn,
           w_gate, w_up, ffn_conv_w, ffn_conv_b, w_down, g_ple, w_ple_gate, w_ple_proj):
    depth = w_in.shape[0]
    row = lambda a: a.reshape(1, -1)
    bias = jnp.asarray(_bias_tables())
    for i in range(depth):
        yc, q, k, v = _proj_call(
            x, row(g_mix[i]), w_in[i].astype(_BF16), conv_w[i], row(conv_b[i]),
            row(jnp.tile(q_norm_g[i], N_HEADS)), row(jnp.tile(k_norm_g[i], N_HEADS)),
            row(g_out_conv[i]))
        ya = _attn_call(q, k, v, bias)
        x = _ffn_call(
            x, yc, ya, p[i], row(g_out_attn[i]), w_out[i].astype(_BF16), row(g_ffn[i]),
            w_gate[i].astype(_BF16), w_up[i].astype(_BF16), ffn_conv_w[i], row(ffn_conv_b[i]),
            w_down[i].astype(_BF16), row(g_ple[i]),
            w_ple_gate[i].astype(_BF16), w_ple_proj[i].astype(_BF16))
    return x
```

```python
import numpy as np
import jax
import jax.numpy as jnp
from jax import lax
from jax.experimental import pallas as pl
from jax.experimental.pallas import tpu as pltpu

D_MODEL = 1024
CONV_W = 512
N_HEADS = 8
HEAD_DIM = 64
ATTN_W = N_HEADS * HEAD_DIM
CONV_K = 3
D_FF = 2816
PLE_DIM = 256
DILATED_PAIRS = ((128, 1), (512, 4), (2048, 16))
BLOCK = 128
EPS = 1e-6

LANES = 128
SUBLANES = 8
HEADS_PER_SLAB = LANES // HEAD_DIM
N_SLABS = ATTN_W // LANES

PROJ_TILE = 1024
TOK_TILE = 512
ATTN_TILE = 2048
FF_CHUNK = 256
N_FF_CHUNKS = D_FF // FF_CHUNK
NEG = -1e30
LOG2E = 1.4426950408889634
VMEM_LIMIT = 56 * 1024 * 1024

_F32 = jnp.float32
_BF16 = jnp.bfloat16


def _rms(x, g):
    ms = jnp.mean(x * x, axis=-1, keepdims=True)
    return x * lax.rsqrt(ms + EPS) * g


def _mm(a, b):
    return jnp.dot(a, b, preferred_element_type=_F32)


def _causal_conv3(u, w, b, carry_ref, work_ref):
    n = u.shape[0]
    work_ref[0:SUBLANES, :] = carry_ref[...]
    work_ref[SUBLANES:SUBLANES + n, :] = u
    u1 = work_ref[SUBLANES - 1:SUBLANES - 1 + n, :]
    u2 = work_ref[SUBLANES - 2:SUBLANES - 2 + n, :]
    carry_ref[...] = work_ref[n:n + SUBLANES, :]
    return u2 * w[0:1, :] + u1 * w[1:2, :] + u * w[2:3, :] + b


def _proj_kernel(x_ref, gmix_ref, win_ref, cw_ref, cb_ref, gq_ref, gk_ref, goc_ref, wg_ref, wu_ref, wd_ref,
                 yc_ref, q_ref, k_ref, v_ref, wg_out_ref, wu_out_ref, wd_out_ref, carry_ref, work_ref):
    @pl.when(pl.program_id(1) == 0)
    def _():
        carry_ref[...] = jnp.zeros_like(carry_ref)

    h = _rms(x_ref[0], gmix_ref[...]).astype(_BF16)

    def proj(c):
        return _mm(h, win_ref[:, c * CONV_W:(c + 1) * CONV_W])

    for src, dst in ((wg_ref, wg_out_ref), (wu_ref, wu_out_ref), (wd_ref, wd_out_ref)):
        dst[...] = src[...].astype(_BF16)

    u = proj(1) * proj(2)
    conv = _causal_conv3(u, cw_ref[...], cb_ref[...], carry_ref, work_ref)
    yc = proj(0) * conv
    yc_ref[0] = _rms(yc, goc_ref[...]).astype(_BF16)

    lane = lax.broadcasted_iota(jnp.int32, (PROJ_TILE, LANES), 1)

    def head_norm(z, g_ref):
        zz = z * z
        slabs = []
        for c in range(N_SLABS):
            blk = zz[:, c * LANES:(c + 1) * LANES]
            ms = jnp.zeros_like(blk)
            for h in range(HEADS_PER_SLAB):
                own = (lane >= h * HEAD_DIM) & (lane < (h + 1) * HEAD_DIM)
                ms = jnp.where(own, jnp.sum(jnp.where(own, blk, 0.0), axis=-1, keepdims=True), ms)
            slabs.append(ms * (1.0 / HEAD_DIM))
        ms = jnp.concatenate(slabs, axis=-1)
        return z * lax.rsqrt(ms + EPS) * g_ref[...]

    q_ref[0] = head_norm(proj(3), gq_ref) * (HEAD_DIM ** -0.5 * LOG2E)
    k_ref[0] = head_norm(proj(4), gk_ref)
    v_ref[0] = proj(5)


def _proj_call(x, g_mix, w_in, conv_w, conv_b, gq, gk, g_out_conv, ffn_weights):
    b, t, _ = x.shape
    steps_per_seq = t // PROJ_TILE
    n_steps = b * steps_per_seq
    const = lambda *shape: pl.BlockSpec(shape, lambda i, j: (0,) * len(shape), pipeline_mode=pl.Buffered(1))
    tile = lambda w: pl.BlockSpec((1, PROJ_TILE, w), lambda i, j: (i, j, 0))

    def row_blocks(w):
        rows, cols = w.shape
        block = next(r for r in range(2 * SUBLANES, rows + 1, 2 * SUBLANES)
                     if rows % r == 0 and rows // r <= n_steps)
        last = rows // block - 1
        return pl.BlockSpec((block, cols), lambda i, j: (jnp.minimum(i * steps_per_seq + j, last), 0))

    w_specs = [row_blocks(w) for w in ffn_weights]
    outs = pl.pallas_call(
        _proj_kernel,
        grid=(b, steps_per_seq),
        in_specs=[tile(D_MODEL), const(1, D_MODEL), const(D_MODEL, 3 * CONV_W + 3 * ATTN_W),
                  const(CONV_K, CONV_W), const(1, CONV_W), const(1, ATTN_W), const(1, ATTN_W),
                  const(1, CONV_W)] + w_specs,
        out_specs=[tile(CONV_W), tile(ATTN_W), tile(ATTN_W), tile(ATTN_W)] + w_specs,
        out_shape=[jax.ShapeDtypeStruct((b, t, CONV_W), _BF16),
                   jax.ShapeDtypeStruct((b, t, ATTN_W), _F32),
                   jax.ShapeDtypeStruct((b, t, ATTN_W), _F32),
                   jax.ShapeDtypeStruct((b, t, ATTN_W), _F32)]
                  + [jax.ShapeDtypeStruct(w.shape, _BF16) for w in ffn_weights],
        scratch_shapes=[pltpu.VMEM((SUBLANES, CONV_W), _F32),
                        pltpu.VMEM((SUBLANES + PROJ_TILE, CONV_W), _F32)],
        compiler_params=pltpu.CompilerParams(
            dimension_semantics=("arbitrary", "arbitrary"), vmem_limit_bytes=VMEM_LIMIT),
        name="proj",
    )(x, g_mix, w_in, conv_w, conv_b, gq, gk, g_out_conv, *ffn_weights)
    return outs[:4], outs[4:]


_BRANCHES = tuple(sorted(DILATED_PAIRS, key=lambda wd: -wd[1]))
_STACK = HEADS_PER_SLAB * BLOCK
assert HEADS_PER_SLAB == 2


_DILATIONS = tuple(d for _, d in _BRANCHES)
assert _DILATIONS == (16, 4, 1)
N_PLANES = _DILATIONS[1]
PLANE_ROWS = ATTN_TILE // N_PLANES
PLANE_STRIDE = _DILATIONS[0] // N_PLANES
PLANE_BLOCKS = PLANE_ROWS // BLOCK
SUB_BLOCKS = PLANE_BLOCKS // PLANE_STRIDE
TILE_BLOCKS = ATTN_TILE // BLOCK
RUN = BLOCK // N_PLANES
_PLANE_ORDER = np.array([N_PLANES * (p % RUN) + p // RUN for p in range(BLOCK)])


def _bias_tables():
    qi = np.arange(BLOCK)[:, None] + BLOCK
    kj = np.arange(2 * BLOCK)[None, :]
    step = (qi - kj).astype(np.float64)
    tabs = np.zeros((N_SLABS, len(_BRANCHES), 2, _STACK, 2 * BLOCK), np.float32)
    for bi, (window, dilation) in enumerate(_BRANCHES):
        band = (step >= 0) & (step <= window // dilation)
        for head in range(N_HEADS):
            slope = 2.0 ** (-8.0 * (head + 1) / N_HEADS)
            slab, h = divmod(head, HEADS_PER_SLAB)
            for variant, keep in enumerate((band, band & (kj >= BLOCK))):
                tab = np.where(keep, -slope * step * dilation * LOG2E, NEG)
                if dilation == 1:
                    tab = tab[_PLANE_ORDER][:, np.concatenate([_PLANE_ORDER, BLOCK + _PLANE_ORDER])]
                tabs[slab, bi, variant, h * BLOCK:(h + 1) * BLOCK] = tab
    return tabs


def _attn_kernel(q_ref, k_ref, v_ref, bias_ref, o_ref, qpl, kpl, vpl, acc_pl, m_pl):
    tile = pl.program_id(2)
    first_tile = tile == 0
    slot = tile % 2
    prev_slot = jnp.where(first_tile, slot, 1 - slot)
    no_prev = first_tile.astype(jnp.int32)
    heads = range(HEADS_PER_SLAB)

    def head_lanes(n_rows):
        lane = lax.broadcasted_iota(jnp.int32, (n_rows, LANES), 1)
        return [(lane >= h * HEAD_DIM) & (lane < (h + 1) * HEAD_DIM) for h in heads]

    def stack(parts):
        return jnp.concatenate(parts, axis=0)

    def split(a):
        return [a[h * BLOCK:(h + 1) * BLOCK] for h in heads]

    def expand_q(q, masks):
        return [jnp.where(masks[h], q, 0.0) for h in heads]

    def expand_v(v, masks):
        return [jnp.where(masks[h], v, 1.0) for h in heads]

    plane_masks = head_lanes(PLANE_ROWS)
    for r in range(N_PLANES):
        plane = pl.ds(r, PLANE_ROWS, stride=N_PLANES)
        qe = expand_q(q_ref[0, plane, :], plane_masks)
        ve = expand_v(v_ref[0, plane, :], plane_masks)
        kpl[slot, r] = k_ref[0, plane, :]
        for h in heads:
            qpl[h, r] = qe[h]
            vpl[slot, h, r] = ve[h]

    def attend(bi, variant, qq, keys, state):
        (k0, v0), (k1, v1) = keys
        s = lax.dot_general(qq, stack([k0, k1]), (((1,), (1,)), ((), ())), preferred_element_type=_F32)
        s = s + bias_ref[0, bi, variant]
        m_new = jnp.broadcast_to(jnp.max(s, axis=-1, keepdims=True), (_STACK, LANES))
        if state is not None:
            m_old, acc_old = state
            m_new = jnp.maximum(m_old, m_new)
            alpha = jnp.exp2(m_old - m_new)
        p = split(jnp.exp2((s - jnp.concatenate([m_new, m_new], axis=1)).astype(_BF16)))
        acc = stack([_mm(p[h], stack([v0[h], v1[h]])) for h in heads])
        if state is not None:
            acc = alpha * acc_old + acc
        return m_new, acc

    def plane_keys(sl, r, rows):
        return kpl[sl, r, rows, :].astype(_BF16), [vpl[sl, h, r, rows, :].astype(_BF16) for h in heads]

    def plane_q(r, rows):
        return stack([qpl[h, r, rows, :] for h in heads]).astype(_BF16)

    def load_state(r, rows):
        return stack([m_pl[h, r, rows, :] for h in heads]), stack([acc_pl[h, r, rows, :] for h in heads])

    def store_state(r, rows, m, acc):
        for h, (ah, mh) in enumerate(zip(split(acc), split(m))):
            acc_pl[h, r, rows, :] = ah
            m_pl[h, r, rows, :] = mh

    for r in range(N_PLANES):
        for a in range(PLANE_STRIDE):
            sub = lambda j, a=a: pl.ds(a + j * BLOCK * PLANE_STRIDE, BLOCK, stride=PLANE_STRIDE)
            keys = [plane_keys(prev_slot, r, sub(SUB_BLOCKS - 1))]
            for j in range(SUB_BLOCKS):
                keys.append(plane_keys(slot, r, sub(j)))
                store_state(r, sub(j), *attend(0, no_prev if j == 0 else 0, plane_q(r, sub(j)),
                                               (keys[j], keys[j + 1]), None))
    def gather(load, j):
        return stack([load(r, pl.ds(j * RUN, RUN)) for r in range(N_PLANES)])

    def token_keys(sl, j):
        k = gather(lambda r, rows: kpl[sl, r, rows, :], j).astype(_BF16)
        return k, [gather(lambda r, rows, h=h: vpl[sl, h, r, rows, :], j).astype(_BF16) for h in heads]

    masks = head_lanes(BLOCK)
    tkeys = [token_keys(prev_slot, TILE_BLOCKS - 1)]

    def token_block(j):
        tkeys.append(token_keys(slot, j))
        qq = stack([gather(lambda r, rows, h=h: qpl[h, r, rows, :], j) for h in heads]).astype(_BF16)
        state = (stack([gather(lambda r, rows, h=h: m_pl[h, r, rows, :], j) for h in heads]),
                 stack([gather(lambda r, rows, h=h: acc_pl[h, r, rows, :], j) for h in heads]))
        _, acc = attend(2, no_prev if j == 0 else 0, qq, (tkeys[j], tkeys[j + 1]), state)
        a0, a1 = split(acc)
        num = jnp.where(masks[0], a0, a1)
        den = pltpu.roll(jnp.where(masks[0], a1, a0), HEAD_DIM, axis=1)
        out = num / den
        for r in range(N_PLANES):
            o_ref[0, pl.ds(j * BLOCK + r, RUN, stride=N_PLANES), :] = out[r * RUN:(r + 1) * RUN]

    keys = [[plane_keys(prev_slot, r, pl.ds(PLANE_ROWS - BLOCK, BLOCK))] for r in range(N_PLANES)]
    for j in range(PLANE_BLOCKS):
        rows = pl.ds(j * BLOCK, BLOCK)
        for r in range(N_PLANES):
            keys[r].append(plane_keys(slot, r, rows))
            store_state(r, rows, *attend(1, no_prev if j == 0 else 0, plane_q(r, rows),
                                         (keys[r][j], keys[r][j + 1]), load_state(r, rows)))
        for jj in range(N_PLANES):
            token_block(j * N_PLANES + jj)


def _attn_call(q, k, v, bias):
    b, t, _ = q.shape
    planes = (N_PLANES, PLANE_ROWS, LANES)
    tile = pl.BlockSpec((1, ATTN_TILE, LANES), lambda i, j, s: (i, s, j))
    return pl.pallas_call(
        _attn_kernel,
        grid=(b, N_SLABS, t // ATTN_TILE),
        in_specs=[tile, tile, tile,
                  pl.BlockSpec((1,) + bias.shape[1:], lambda i, j, s: (j, 0, 0, 0, 0))],
        out_specs=tile,
        out_shape=jax.ShapeDtypeStruct((b, t, ATTN_W), _F32),
        scratch_shapes=[pltpu.VMEM((HEADS_PER_SLAB,) + planes, _F32),
                        pltpu.VMEM((2,) + planes, _F32),
                        pltpu.VMEM((2, HEADS_PER_SLAB) + planes, _F32),
                        pltpu.VMEM((HEADS_PER_SLAB,) + planes, _F32),
                        pltpu.VMEM((HEADS_PER_SLAB,) + planes, _F32)],
        compiler_params=pltpu.CompilerParams(
            dimension_semantics=("arbitrary", "arbitrary", "arbitrary"), vmem_limit_bytes=VMEM_LIMIT),
        name="attn",
    )(q, k, v, bias)


def _ffn_kernel(x_ref, yc_ref, ya_ref, p_ref, goa_ref, wout_ref, gffn_ref, wg_ref, wu_ref, fcw_ref, fcb_ref,
                wd_ref, gple_ref, wpg_ref, wpp_ref, o_ref, x1_ref, act_ref, carry_ref, work_ref):
    @pl.when(pl.program_id(1) == 0)
    def _():
        carry_ref[...] = jnp.zeros_like(carry_ref)

    ya = _rms(ya_ref[0], goa_ref[...]).astype(_BF16)
    x1 = x_ref[0] + _mm(yc_ref[0], wout_ref[0:CONV_W, :]) + _mm(ya, wout_ref[CONV_W:CONV_W + ATTN_W, :])
    x1_ref[...] = x1
    h = _rms(x1, gffn_ref[...]).astype(_BF16)

    def gate_up(c):
        cols = slice(c * FF_CHUNK, (c + 1) * FF_CHUNK)
        return _mm(h, jnp.concatenate([wg_ref[:, cols], wu_ref[:, cols]], axis=1))

    gu_next = gate_up(0)
    for c in range(N_FF_CHUNKS):
        gu = gu_next
        if c + 1 < N_FF_CHUNKS:
            gu_next = gate_up(c + 1)
        cols = slice(c * FF_CHUNK, (c + 1) * FF_CHUNK)
        gate = _causal_conv3(gu[:, :FF_CHUNK], fcw_ref[:, cols], fcb_ref[:, cols], carry_ref.at[c],
                             work_ref.at[c % 2])
        act = gate / (1.0 + jnp.exp(-gate)) * gu[:, FF_CHUNK:]
        act_ref[:, cols] = act.astype(_BF16)

    x2 = x1_ref[...] + _mm(act_ref[...], wd_ref[...])
    h3 = _rms(x2, gple_ref[...]).astype(_BF16)
    gate = 1.0 / (1.0 + jnp.exp(-_mm(h3, wpg_ref[...])))
    o_ref[0] = x2 + gate * _mm(p_ref[0].astype(_BF16), wpp_ref[...])


def _ffn_call(x, yc, ya, p, g_out_attn, w_out, g_ffn, wg, wu, fcw, fcb, wd, g_ple, wpg, wpp):
    b, t, _ = x.shape

    def const(*shape):
        return pl.BlockSpec(shape, lambda i, j: (0,) * len(shape), pipeline_mode=pl.Buffered(1))

    tile = lambda w: pl.BlockSpec((1, TOK_TILE, w), lambda i, j: (i, j, 0))
    return pl.pallas_call(
        _ffn_kernel,
        grid=(b, t // TOK_TILE),
        in_specs=[tile(D_MODEL), tile(CONV_W), tile(ATTN_W), tile(PLE_DIM),
                  const(1, ATTN_W), const(CONV_W + ATTN_W, D_MODEL), const(1, D_MODEL),
                  const(D_MODEL, D_FF), const(D_MODEL, D_FF), const(CONV_K, D_FF), const(1, D_FF),
                  const(D_FF, D_MODEL), const(1, D_MODEL),
                  const(D_MODEL, D_MODEL), const(PLE_DIM, D_MODEL)],
        out_specs=tile(D_MODEL),
        out_shape=jax.ShapeDtypeStruct((b, t, D_MODEL), _F32),
        scratch_shapes=[pltpu.VMEM((TOK_TILE, D_MODEL), _F32),
                        pltpu.VMEM((TOK_TILE, D_FF), _BF16),
                        pltpu.VMEM((N_FF_CHUNKS, SUBLANES, FF_CHUNK), _F32),
                        pltpu.VMEM((2, SUBLANES + TOK_TILE, FF_CHUNK), _F32)],
        compiler_params=pltpu.CompilerParams(
            dimension_semantics=("arbitrary", "arbitrary"), vmem_limit_bytes=VMEM_LIMIT),
        name="ffn",
    )(x, yc, ya, p, g_out_attn, w_out, g_ffn, wg, wu, fcw, fcb, wd, g_ple, wpg, wpp)


def kernel(x, p, g_mix, w_in, conv_w, conv_b, q_norm_g, k_norm_g, g_out_conv, g_out_attn, w_out, g_ffn,
           w_gate, w_up, ffn_conv_w, ffn_conv_b, w_down, g_ple, w_ple_gate, w_ple_proj):
    depth = w_in.shape[0]
    row = lambda a: a.reshape(1, -1)
    bias = jnp.asarray(_bias_tables())
    for i in range(depth):
        (yc, q, k, v), (wg, wu, wd) = _proj_call(
            x, row(g_mix[i]), w_in[i].astype(_BF16), conv_w[i], row(conv_b[i]),
            row(jnp.tile(q_norm_g[i], N_HEADS)), row(jnp.tile(k_norm_g[i], N_HEADS)),
            row(g_out_conv[i]), (w_gate[i], w_up[i], w_down[i]))
        ya = _attn_call(q, k, v, bias)
        x = _ffn_call(
            x, yc, ya, p[i], row(g_out_attn[i]), w_out[i].astype(_BF16), row(g_ffn[i]),
            wg, wu, ffn_conv_w[i], row(ffn_conv_b[i]), wd, row(g_ple[i]),
            w_ple_gate[i].astype(_BF16), w_ple_proj[i].astype(_BF16))
    return x
```

```python
import numpy as np
import jax
import jax.numpy as jnp
from jax import lax
from jax.experimental import pallas as pl
from jax.experimental.pallas import tpu as pltpu

D_MODEL = 1024
CONV_W = 512
N_HEADS = 8
HEAD_DIM = 64
ATTN_W = N_HEADS * HEAD_DIM
CONV_K = 3
D_FF = 2816
PLE_DIM = 256
DILATED_PAIRS = ((128, 1), (512, 4), (2048, 16))
BLOCK = 128
EPS = 1e-6

LANES = 128
SUBLANES = 8
HEADS_PER_SLAB = LANES // HEAD_DIM
N_SLABS = ATTN_W // LANES

PROJ_TILE = 1024
TOK_TILE = 512
ATTN_TILE = 2048
FF_CHUNK = 256
N_FF_CHUNKS = D_FF // FF_CHUNK
NEG = -1e30
LOG2E = 1.4426950408889634
VMEM_LIMIT = 56 * 1024 * 1024

_F32 = jnp.float32
_BF16 = jnp.bfloat16


def _rms(x, g):
    ms = jnp.mean(x * x, axis=-1, keepdims=True)
    return x * lax.rsqrt(ms + EPS) * g


def _mm(a, b):
    return jnp.dot(a, b, preferred_element_type=_F32)


def _causal_conv3(u, w, b, carry_ref, work_ref):
    n = u.shape[0]
    work_ref[0:SUBLANES, :] = carry_ref[...]
    work_ref[SUBLANES:SUBLANES + n, :] = u
    u1 = work_ref[SUBLANES - 1:SUBLANES - 1 + n, :]
    u2 = work_ref[SUBLANES - 2:SUBLANES - 2 + n, :]
    carry_ref[...] = work_ref[n:n + SUBLANES, :]
    return u2 * w[0:1, :] + u1 * w[1:2, :] + u * w[2:3, :] + b


N_PROJ_OUTPUTS = 4


def _proj_kernel(x_ref, gmix_ref, win_ref, cw_ref, cb_ref, gq_ref, gk_ref, goc_ref, *refs):
    n_w = (len(refs) - N_PROJ_OUTPUTS - 2) // 2
    w_refs, (yc_ref, q_ref, k_ref, v_ref) = refs[:n_w], refs[n_w:n_w + N_PROJ_OUTPUTS]
    w_out_refs, (carry_ref, work_ref) = refs[n_w + N_PROJ_OUTPUTS:-2], refs[-2:]

    @pl.when(pl.program_id(1) == 0)
    def _():
        carry_ref[...] = jnp.zeros_like(carry_ref)

    h = _rms(x_ref[0], gmix_ref[...]).astype(_BF16)

    def proj(c):
        return _mm(h, win_ref[:, c * CONV_W:(c + 1) * CONV_W])

    for src, dst in zip(w_refs, w_out_refs):
        dst[...] = src[...].astype(_BF16)

    u = proj(1) * proj(2)
    conv = _causal_conv3(u, cw_ref[...], cb_ref[...], carry_ref, work_ref)
    yc = proj(0) * conv
    yc_ref[0] = _rms(yc, goc_ref[...]).astype(_BF16)

    lane = lax.broadcasted_iota(jnp.int32, (PROJ_TILE, LANES), 1)

    def head_norm(z, g_ref):
        zz = z * z
        slabs = []
        for c in range(N_SLABS):
            blk = zz[:, c * LANES:(c + 1) * LANES]
            ms = jnp.zeros_like(blk)
            for h in range(HEADS_PER_SLAB):
                own = (lane >= h * HEAD_DIM) & (lane < (h + 1) * HEAD_DIM)
                ms = jnp.where(own, jnp.sum(jnp.where(own, blk, 0.0), axis=-1, keepdims=True), ms)
            slabs.append(ms * (1.0 / HEAD_DIM))
        ms = jnp.concatenate(slabs, axis=-1)
        return z * lax.rsqrt(ms + EPS) * g_ref[...]

    q_ref[0] = head_norm(proj(3), gq_ref) * (HEAD_DIM ** -0.5 * LOG2E)
    k_ref[0] = head_norm(proj(4), gk_ref)
    v_ref[0] = proj(5)


def _proj_call(x, g_mix, w_in, conv_w, conv_b, gq, gk, g_out_conv, ffn_weights):
    b, t, _ = x.shape
    steps_per_seq = t // PROJ_TILE
    n_steps = b * steps_per_seq
    const = lambda *shape: pl.BlockSpec(shape, lambda i, j: (0,) * len(shape), pipeline_mode=pl.Buffered(1))
    tile = lambda w: pl.BlockSpec((1, PROJ_TILE, w), lambda i, j: (i, j, 0))

    def row_blocks(w):
        rows, cols = w.shape
        block = next(r for r in range(2 * SUBLANES, rows + 1, 2 * SUBLANES)
                     if rows % r == 0 and rows // r <= n_steps)
        last = rows // block - 1
        return pl.BlockSpec((block, cols), lambda i, j: (jnp.minimum(i * steps_per_seq + j, last), 0))

    w_specs = [row_blocks(w) for w in ffn_weights]
    outs = pl.pallas_call(
        _proj_kernel,
        grid=(b, steps_per_seq),
        in_specs=[tile(D_MODEL), const(1, D_MODEL), const(D_MODEL, 3 * CONV_W + 3 * ATTN_W),
                  const(CONV_K, CONV_W), const(1, CONV_W), const(1, ATTN_W), const(1, ATTN_W),
                  const(1, CONV_W)] + w_specs,
        out_specs=[tile(CONV_W), tile(ATTN_W), tile(ATTN_W), tile(ATTN_W)] + w_specs,
        out_shape=[jax.ShapeDtypeStruct((b, t, CONV_W), _BF16),
                   jax.ShapeDtypeStruct((b, t, ATTN_W), _F32),
                   jax.ShapeDtypeStruct((b, t, ATTN_W), _F32),
                   jax.ShapeDtypeStruct((b, t, ATTN_W), _F32)]
                  + [jax.ShapeDtypeStruct(w.shape, _BF16) for w in ffn_weights],
        scratch_shapes=[pltpu.VMEM((SUBLANES, CONV_W), _F32),
                        pltpu.VMEM((SUBLANES + PROJ_TILE, CONV_W), _F32)],
        compiler_params=pltpu.CompilerParams(
            dimension_semantics=("arbitrary", "arbitrary"), vmem_limit_bytes=VMEM_LIMIT),
        name="proj",
    )(x, g_mix, w_in, conv_w, conv_b, gq, gk, g_out_conv, *ffn_weights)
    return outs[:N_PROJ_OUTPUTS], outs[N_PROJ_OUTPUTS:]


_BRANCHES = tuple(sorted(DILATED_PAIRS, key=lambda wd: -wd[1]))
_STACK = HEADS_PER_SLAB * BLOCK
assert HEADS_PER_SLAB == 2

_DILATIONS = tuple(d for _, d in _BRANCHES)
assert _DILATIONS == (16, 4, 1)
N_PLANES = _DILATIONS[1]
PLANE_ROWS = ATTN_TILE // N_PLANES
PLANE_STRIDE = _DILATIONS[0] // N_PLANES
PLANE_BLOCKS = PLANE_ROWS // BLOCK
SUB_BLOCKS = PLANE_BLOCKS // PLANE_STRIDE
TILE_BLOCKS = ATTN_TILE // BLOCK
RUN = BLOCK // N_PLANES
_PLANE_ORDER = np.array([N_PLANES * (p % RUN) + p // RUN for p in range(BLOCK)])


def _bias_tables():
    qi = np.arange(BLOCK)[:, None] + BLOCK
    kj = np.arange(2 * BLOCK)[None, :]
    step = (qi - kj).astype(np.float64)
    tabs = np.zeros((N_SLABS, len(_BRANCHES), 2, _STACK, 2 * BLOCK), np.float32)
    for bi, (window, dilation) in enumerate(_BRANCHES):
        band = (step >= 0) & (step <= window // dilation)
        for head in range(N_HEADS):
            slope = 2.0 ** (-8.0 * (head + 1) / N_HEADS)
            slab, h = divmod(head, HEADS_PER_SLAB)
            for variant, keep in enumerate((band, band & (kj >= BLOCK))):
                tab = np.where(keep, -slope * step * dilation * LOG2E, NEG)
                if dilation == 1:
                    tab = tab[_PLANE_ORDER][:, np.concatenate([_PLANE_ORDER, BLOCK + _PLANE_ORDER])]
                tabs[slab, bi, variant, h * BLOCK:(h + 1) * BLOCK] = tab
    return tabs


def _attn_kernel(q_ref, k_ref, v_ref, bias_ref, o_ref, qpl, kpl, vpl, acc_pl, m_pl):
    tile = pl.program_id(2)
    first_tile = tile == 0
    slot = tile % 2
    prev_slot = jnp.where(first_tile, slot, 1 - slot)
    no_prev = first_tile.astype(jnp.int32)
    heads = range(HEADS_PER_SLAB)

    def head_lanes(n_rows):
        lane = lax.broadcasted_iota(jnp.int32, (n_rows, LANES), 1)
        return [(lane >= h * HEAD_DIM) & (lane < (h + 1) * HEAD_DIM) for h in heads]

    def stack(parts):
        return jnp.concatenate(parts, axis=0)

    def split(a):
        return [a[h * BLOCK:(h + 1) * BLOCK] for h in heads]

    def expand_q(q, masks):
        return [jnp.where(masks[h], q, 0.0) for h in heads]

    def expand_v(v, masks):
        return [jnp.where(masks[h], v, 1.0) for h in heads]

    plane_masks = head_lanes(PLANE_ROWS)
    for r in range(N_PLANES):
        plane = pl.ds(r, PLANE_ROWS, stride=N_PLANES)
        qe = expand_q(q_ref[0, plane, :], plane_masks)
        ve = expand_v(v_ref[0, plane, :], plane_masks)
        kpl[slot, r] = k_ref[0, plane, :]
        for h in heads:
            qpl[h, r] = qe[h]
            vpl[slot, h, r] = ve[h]

    def attend(bi, variant, qq, keys, state):
        (k0, v0), (k1, v1) = keys
        s = lax.dot_general(qq, stack([k0, k1]), (((1,), (1,)), ((), ())), preferred_element_type=_F32)
        s = s + bias_ref[0, bi, variant]
        m_new = jnp.broadcast_to(jnp.max(s, axis=-1, keepdims=True), (_STACK, LANES))
        if state is not None:
            m_old, acc_old = state
            m_new = jnp.maximum(m_old, m_new)
            alpha = jnp.exp2(m_old - m_new)
        p = split(jnp.exp2((s - jnp.concatenate([m_new, m_new], axis=1)).astype(_BF16)))
        acc = stack([_mm(p[h], stack([v0[h], v1[h]])) for h in heads])
        if state is not None:
            acc = alpha * acc_old + acc
        return m_new, acc

    def plane_keys(sl, r, rows):
        return kpl[sl, r, rows, :].astype(_BF16), [vpl[sl, h, r, rows, :].astype(_BF16) for h in heads]

    def plane_q(r, rows):
        return stack([qpl[h, r, rows, :] for h in heads]).astype(_BF16)

    def load_state(r, rows):
        return stack([m_pl[h, r, rows, :] for h in heads]), stack([acc_pl[h, r, rows, :] for h in heads])

    def store_state(r, rows, m, acc):
        for h, (ah, mh) in enumerate(zip(split(acc), split(m))):
            acc_pl[h, r, rows, :] = ah
            m_pl[h, r, rows, :] = mh

    for r in range(N_PLANES):
        for a in range(PLANE_STRIDE):
            sub = lambda j, a=a: pl.ds(a + j * BLOCK * PLANE_STRIDE, BLOCK, stride=PLANE_STRIDE)
            keys = [plane_keys(prev_slot, r, sub(SUB_BLOCKS - 1))]
            for j in range(SUB_BLOCKS):
                keys.append(plane_keys(slot, r, sub(j)))
                store_state(r, sub(j), *attend(0, no_prev if j == 0 else 0, plane_q(r, sub(j)),
                                               (keys[j], keys[j + 1]), None))

    def gather(load, j):
        return stack([load(r, pl.ds(j * RUN, RUN)) for r in range(N_PLANES)])

    def token_keys(sl, j):
        k = gather(lambda r, rows: kpl[sl, r, rows, :], j).astype(_BF16)
        return k, [gather(lambda r, rows, h=h: vpl[sl, h, r, rows, :], j).astype(_BF16) for h in heads]

    masks = head_lanes(BLOCK)
    tkeys = [token_keys(prev_slot, TILE_BLOCKS - 1)]

    def token_block(j):
        tkeys.append(token_keys(slot, j))
        qq = stack([gather(lambda r, rows, h=h: qpl[h, r, rows, :], j) for h in heads]).astype(_BF16)
        state = (stack([gather(lambda r, rows, h=h: m_pl[h, r, rows, :], j) for h in heads]),
                 stack([gather(lambda r, rows, h=h: acc_pl[h, r, rows, :], j) for h in heads]))
        _, acc = attend(2, no_prev if j == 0 else 0, qq, (tkeys[j], tkeys[j + 1]), state)
        a0, a1 = split(acc)
        num = jnp.where(masks[0], a0, a1)
        den = pltpu.roll(jnp.where(masks[0], a1, a0), HEAD_DIM, axis=1)
        out = num / den
        for r in range(N_PLANES):
            o_ref[0, pl.ds(j * BLOCK + r, RUN, stride=N_PLANES), :] = out[r * RUN:(r + 1) * RUN]

    keys = [[plane_keys(prev_slot, r, pl.ds(PLANE_ROWS - BLOCK, BLOCK))] for r in range(N_PLANES)]
    for j in range(PLANE_BLOCKS):
        rows = pl.ds(j * BLOCK, BLOCK)
        for r in range(N_PLANES):
            keys[r].append(plane_keys(slot, r, rows))
            store_state(r, rows, *attend(1, no_prev if j == 0 else 0, plane_q(r, rows),
                                         (keys[r][j], keys[r][j + 1]), load_state(r, rows)))
        for jj in range(N_PLANES):
            token_block(j * N_PLANES + jj)


def _attn_call(q, k, v, bias):
    b, t, _ = q.shape
    planes = (N_PLANES, PLANE_ROWS, LANES)
    tile = pl.BlockSpec((1, ATTN_TILE, LANES), lambda i, j, s: (i, s, j))
    return pl.pallas_call(
        _attn_kernel,
        grid=(b, N_SLABS, t // ATTN_TILE),
        in_specs=[tile, tile, tile,
                  pl.BlockSpec((1,) + bias.shape[1:], lambda i, j, s: (j, 0, 0, 0, 0))],
        out_specs=tile,
        out_shape=jax.ShapeDtypeStruct((b, t, ATTN_W), _F32),
        scratch_shapes=[pltpu.VMEM((HEADS_PER_SLAB,) + planes, _F32),
                        pltpu.VMEM((2,) + planes, _F32),
                        pltpu.VMEM((2, HEADS_PER_SLAB) + planes, _F32),
                        pltpu.VMEM((HEADS_PER_SLAB,) + planes, _F32),
                        pltpu.VMEM((HEADS_PER_SLAB,) + planes, _F32)],
        compiler_params=pltpu.CompilerParams(
            dimension_semantics=("arbitrary", "arbitrary", "arbitrary"), vmem_limit_bytes=VMEM_LIMIT),
        name="attn",
    )(q, k, v, bias)


def _ffn_kernel(x_ref, yc_ref, ya_ref, p_ref, goa_ref, wout_ref, gffn_ref, wg_ref, wu_ref, fcw_ref, fcb_ref,
                wd_ref, gple_ref, wpg_ref, wpp_ref, o_ref, x1_ref, act_ref, carry_ref, work_ref):
    @pl.when(pl.program_id(1) == 0)
    def _():
        carry_ref[...] = jnp.zeros_like(carry_ref)

    ya = _rms(ya_ref[0], goa_ref[...]).astype(_BF16)
    x1 = x_ref[0] + _mm(yc_ref[0], wout_ref[0:CONV_W, :]) + _mm(ya, wout_ref[CONV_W:CONV_W + ATTN_W, :])
    x1_ref[...] = x1
    h = _rms(x1, gffn_ref[...]).astype(_BF16)

    def gate_up(c):
        cols = slice(c * FF_CHUNK, (c + 1) * FF_CHUNK)
        return _mm(h, jnp.concatenate([wg_ref[:, cols], wu_ref[:, cols]], axis=1))

    gu_next = gate_up(0)
    for c in range(N_FF_CHUNKS):
        gu = gu_next
        if c + 1 < N_FF_CHUNKS:
            gu_next = gate_up(c + 1)
        cols = slice(c * FF_CHUNK, (c + 1) * FF_CHUNK)
        gate = _causal_conv3(gu[:, :FF_CHUNK], fcw_ref[:, cols], fcb_ref[:, cols], carry_ref.at[c],
                             work_ref.at[c % 2])
        act = gate / (1.0 + jnp.exp(-gate)) * gu[:, FF_CHUNK:]
        act_ref[:, cols] = act.astype(_BF16)

    x2 = x1_ref[...] + _mm(act_ref[...], wd_ref[...])
    h3 = _rms(x2, gple_ref[...]).astype(_BF16)
    gate = 1.0 / (1.0 + jnp.exp(-_mm(h3, wpg_ref[...])))
    o_ref[0] = x2 + gate * _mm(p_ref[0].astype(_BF16), wpp_ref[...])


def _ffn_call(x, yc, ya, p, g_out_attn, w_out, g_ffn, wg, wu, fcw, fcb, wd, g_ple, wpg, wpp):
    b, t, _ = x.shape

    def const(*shape):
        return pl.BlockSpec(shape, lambda i, j: (0,) * len(shape), pipeline_mode=pl.Buffered(1))

    tile = lambda w: pl.BlockSpec((1, TOK_TILE, w), lambda i, j: (i, j, 0))
    return pl.pallas_call(
        _ffn_kernel,
        grid=(b, t // TOK_TILE),
        in_specs=[tile(D_MODEL), tile(CONV_W), tile(ATTN_W), tile(PLE_DIM),
                  const(1, ATTN_W), const(CONV_W + ATTN_W, D_MODEL), const(1, D_MODEL),
                  const(D_MODEL, D_FF), const(D_MODEL, D_FF), const(CONV_K, D_FF), const(1, D_FF),
                  const(D_FF, D_MODEL), const(1, D_MODEL),
                  const(D_MODEL, D_MODEL), const(PLE_DIM, D_MODEL)],
        out_specs=tile(D_MODEL),
        out_shape=jax.ShapeDtypeStruct((b, t, D_MODEL), _F32),
        scratch_shapes=[pltpu.VMEM((TOK_TILE, D_MODEL), _F32),
                        pltpu.VMEM((TOK_TILE, D_FF), _BF16),
                        pltpu.VMEM((N_FF_CHUNKS, SUBLANES, FF_CHUNK), _F32),
                        pltpu.VMEM((2, SUBLANES + TOK_TILE, FF_CHUNK), _F32)],
        compiler_params=pltpu.CompilerParams(
            dimension_semantics=("arbitrary", "arbitrary"), vmem_limit_bytes=VMEM_LIMIT),
        name="ffn",
    )(x, yc, ya, p, g_out_attn, w_out, g_ffn, wg, wu, fcw, fcb, wd, g_ple, wpg, wpp)


def kernel(x, p, g_mix, w_in, conv_w, conv_b, q_norm_g, k_norm_g, g_out_conv, g_out_attn, w_out, g_ffn,
           w_gate, w_up, ffn_conv_w, ffn_conv_b, w_down, g_ple, w_ple_gate, w_ple_proj):
    depth = w_in.shape[0]
    row = lambda a: a.reshape(1, -1)
    bias = jnp.asarray(_bias_tables())
    for i in range(depth):
        (yc, q, k, v), (wg, wu, wd, wout, wpg, wpp) = _proj_call(
            x, row(g_mix[i]), w_in[i].astype(_BF16), conv_w[i], row(conv_b[i]),
            row(jnp.tile(q_norm_g[i], N_HEADS)), row(jnp.tile(k_norm_g[i], N_HEADS)),
            row(g_out_conv[i]), (w_gate[i], w_up[i], w_down[i], w_out[i], w_ple_gate[i], w_ple_proj[i]))
        ya = _attn_call(q, k, v, bias)
        x = _ffn_call(
            x, yc, ya, p[i], row(g_out_attn[i]), wout, row(g_ffn[i]),
            wg, wu, ffn_conv_w[i], row(ffn_conv_b[i]), wd, row(g_ple[i]), wpg, wpp)
    return x
```

```python
import numpy as np
import jax
import jax.numpy as jnp
from jax import lax
from jax.experimental import pallas as pl
from jax.experimental.pallas import tpu as pltpu

D_MODEL = 1024
CONV_W = 512
N_HEADS = 8
HEAD_DIM = 64
ATTN_W = N_HEADS * HEAD_DIM
CONV_K = 3
D_FF = 2816
PLE_DIM = 256
DILATED_PAIRS = ((128, 1), (512, 4), (2048, 16))
BLOCK = 128
EPS = 1e-6

LANES = 128
SUBLANES = 8
HEADS_PER_SLAB = LANES // HEAD_DIM
N_SLABS = ATTN_W // LANES

PROJ_TILE = 1024
TOK_TILE = 512
ATTN_TILE = 2048
FF_CHUNK = 256
N_FF_CHUNKS = D_FF // FF_CHUNK
NEG = -1e30
LOG2E = 1.4426950408889634
VMEM_LIMIT = 56 * 1024 * 1024

_F32 = jnp.float32
_BF16 = jnp.bfloat16


def _rms(x, g):
    ms = jnp.mean(x * x, axis=-1, keepdims=True)
    return x * lax.rsqrt(ms + EPS) * g


def _mm(a, b):
    return jnp.dot(a, b, preferred_element_type=_F32)


def _causal_conv3(u, w, b, carry_ref, work_ref):
    n = u.shape[0]
    work_ref[0:SUBLANES, :] = carry_ref[...]
    work_ref[SUBLANES:SUBLANES + n, :] = u
    u1 = work_ref[SUBLANES - 1:SUBLANES - 1 + n, :]
    u2 = work_ref[SUBLANES - 2:SUBLANES - 2 + n, :]
    carry_ref[...] = work_ref[n:n + SUBLANES, :]
    return u2 * w[0:1, :] + u1 * w[1:2, :] + u * w[2:3, :] + b


N_PROJ_OUTPUTS = 4


def _proj_kernel(x_ref, gmix_ref, win_ref, cw_ref, cb_ref, gq_ref, gk_ref, goc_ref, *refs):
    n_w = (len(refs) - N_PROJ_OUTPUTS - 2) // 2
    w_refs, (yc_ref, q_ref, k_ref, v_ref) = refs[:n_w], refs[n_w:n_w + N_PROJ_OUTPUTS]
    w_out_refs, (carry_ref, work_ref) = refs[n_w + N_PROJ_OUTPUTS:-2], refs[-2:]

    @pl.when(pl.program_id(1) == 0)
    def _():
        carry_ref[...] = jnp.zeros_like(carry_ref)

    h = _rms(x_ref[0], gmix_ref[...]).astype(_BF16)

    def proj(c):
        return _mm(h, win_ref[:, c * CONV_W:(c + 1) * CONV_W])

    for src, dst in zip(w_refs, w_out_refs):
        dst[...] = src[...].astype(_BF16)

    u = proj(1) * proj(2)
    conv = _causal_conv3(u, cw_ref[...], cb_ref[...], carry_ref, work_ref)
    yc = proj(0) * conv
    yc_ref[0] = _rms(yc, goc_ref[...]).astype(_BF16)

    lane = lax.broadcasted_iota(jnp.int32, (PROJ_TILE, LANES), 1)

    def head_norm(z, g_ref):
        zz = z * z
        slabs = []
        for c in range(N_SLABS):
            blk = zz[:, c * LANES:(c + 1) * LANES]
            ms = jnp.zeros_like(blk)
            for h in range(HEADS_PER_SLAB):
                own = (lane >= h * HEAD_DIM) & (lane < (h + 1) * HEAD_DIM)
                ms = jnp.where(own, jnp.sum(jnp.where(own, blk, 0.0), axis=-1, keepdims=True), ms)
            slabs.append(ms * (1.0 / HEAD_DIM))
        ms = jnp.concatenate(slabs, axis=-1)
        return z * lax.rsqrt(ms + EPS) * g_ref[...]

    q_ref[0] = head_norm(proj(3), gq_ref) * (HEAD_DIM ** -0.5 * LOG2E)
    k_ref[0] = head_norm(proj(4), gk_ref)
    v_ref[0] = proj(5)


def _proj_call(x, g_mix, w_in, conv_w, conv_b, gq, gk, g_out_conv, ffn_weights):
    b, t, _ = x.shape
    steps_per_seq = t // PROJ_TILE
    n_steps = b * steps_per_seq
    const = lambda *shape: pl.BlockSpec(shape, lambda i, j: (0,) * len(shape), pipeline_mode=pl.Buffered(1))
    tile = lambda w: pl.BlockSpec((1, PROJ_TILE, w), lambda i, j: (i, j, 0))

    def row_blocks(w):
        rows, cols = w.shape
        block = next(r for r in range(2 * SUBLANES, rows + 1, 2 * SUBLANES)
                     if rows % r == 0 and rows // r <= n_steps)
        last = rows // block - 1
        return pl.BlockSpec((block, cols), lambda i, j: (jnp.minimum(i * steps_per_seq + j, last), 0))

    w_specs = [row_blocks(w) for w in ffn_weights]
    outs = pl.pallas_call(
        _proj_kernel,
        grid=(b, steps_per_seq),
        in_specs=[tile(D_MODEL), const(1, D_MODEL), const(D_MODEL, 3 * CONV_W + 3 * ATTN_W),
                  const(CONV_K, CONV_W), const(1, CONV_W), const(1, ATTN_W), const(1, ATTN_W),
                  const(1, CONV_W)] + w_specs,
        out_specs=[tile(CONV_W), tile(ATTN_W), tile(ATTN_W), tile(ATTN_W)] + w_specs,
        out_shape=[jax.ShapeDtypeStruct((b, t, CONV_W), _BF16),
                   jax.ShapeDtypeStruct((b, t, ATTN_W), _F32),
                   jax.ShapeDtypeStruct((b, t, ATTN_W), _F32),
                   jax.ShapeDtypeStruct((b, t, ATTN_W), _F32)]
                  + [jax.ShapeDtypeStruct(w.shape, _BF16) for w in ffn_weights],
        scratch_shapes=[pltpu.VMEM((SUBLANES, CONV_W), _F32),
                        pltpu.VMEM((SUBLANES + PROJ_TILE, CONV_W), _F32)],
        compiler_params=pltpu.CompilerParams(
            dimension_semantics=("arbitrary", "arbitrary"), vmem_limit_bytes=VMEM_LIMIT),
        name="proj",
    )(x, g_mix, w_in, conv_w, conv_b, gq, gk, g_out_conv, *ffn_weights)
    return outs[:N_PROJ_OUTPUTS], outs[N_PROJ_OUTPUTS:]


_BRANCHES = tuple(sorted(DILATED_PAIRS, key=lambda wd: -wd[1]))
_STACK = HEADS_PER_SLAB * BLOCK
assert HEADS_PER_SLAB == 2

_DILATIONS = tuple(d for _, d in _BRANCHES)
assert _DILATIONS == (16, 4, 1)
N_PLANES = _DILATIONS[1]
PLANE_ROWS = ATTN_TILE // N_PLANES
PLANE_STRIDE = _DILATIONS[0] // N_PLANES
PLANE_BLOCKS = PLANE_ROWS // BLOCK
SUB_BLOCKS = PLANE_BLOCKS // PLANE_STRIDE
TILE_BLOCKS = ATTN_TILE // BLOCK
RUN = BLOCK // N_PLANES
_PLANE_ORDER = np.array([N_PLANES * (p % RUN) + p // RUN for p in range(BLOCK)])


def _bias_tables():
    qi = np.arange(BLOCK)[:, None] + BLOCK
    kj = np.arange(2 * BLOCK)[None, :]
    step = (qi - kj).astype(np.float64)
    tabs = np.zeros((N_SLABS, len(_BRANCHES), 2, _STACK, 2 * BLOCK), np.float32)
    for bi, (window, dilation) in enumerate(_BRANCHES):
        band = (step >= 0) & (step <= window // dilation)
        for head in range(N_HEADS):
            slope = 2.0 ** (-8.0 * (head + 1) / N_HEADS)
            slab, h = divmod(head, HEADS_PER_SLAB)
            for variant, keep in enumerate((band, band & (kj >= BLOCK))):
                tab = np.where(keep, -slope * step * dilation * LOG2E, NEG)
                if dilation == 1:
                    tab = tab[_PLANE_ORDER][:, np.concatenate([_PLANE_ORDER, BLOCK + _PLANE_ORDER])]
                tabs[slab, bi, variant, h * BLOCK:(h + 1) * BLOCK] = tab
    return tabs


def _attn_kernel(q_ref, k_ref, v_ref, bias_ref, o_ref, qpl, kpl, vpl, acc_pl, m_pl):
    tile = pl.program_id(2)
    first_tile = tile == 0
    slot = tile % 2
    prev_slot = jnp.where(first_tile, slot, 1 - slot)
    no_prev = first_tile.astype(jnp.int32)
    heads = range(HEADS_PER_SLAB)

    def head_lanes(n_rows):
        lane = lax.broadcasted_iota(jnp.int32, (n_rows, LANES), 1)
        return [(lane >= h * HEAD_DIM) & (lane < (h + 1) * HEAD_DIM) for h in heads]

    def stack(parts):
        return jnp.concatenate(parts, axis=0)

    def split(a):
        return [a[h * BLOCK:(h + 1) * BLOCK] for h in heads]

    def expand_q(q, masks):
        return [jnp.where(masks[h], q, 0.0) for h in heads]

    def expand_v(v, masks):
        return [jnp.where(masks[h], v, 1.0) for h in heads]

    plane_masks = head_lanes(PLANE_ROWS)
    for r in range(N_PLANES):
        plane = pl.ds(r, PLANE_ROWS, stride=N_PLANES)
        qe = expand_q(q_ref[0, plane, :], plane_masks)
        ve = expand_v(v_ref[0, plane, :], plane_masks)
        kpl[slot, r] = k_ref[0, plane, :]
        for h in heads:
            qpl[h, r] = qe[h]
            vpl[slot, h, r] = ve[h]

    def attend(bi, variant, qq, keys, state):
        (k0, v0), (k1, v1) = keys
        s = lax.dot_general(qq, stack([k0, k1]), (((1,), (1,)), ((), ())), preferred_element_type=_F32)
        s = s + bias_ref[0, bi, variant]
        m_new = jnp.broadcast_to(jnp.max(s, axis=-1, keepdims=True), (_STACK, LANES))
        if state is not None:
            m_old, acc_old = state
            m_new = jnp.maximum(m_old, m_new)
            alpha = jnp.exp2(m_old - m_new)
        p = split(jnp.exp2((s - jnp.concatenate([m_new, m_new], axis=1)).astype(_BF16)))
        acc = stack([_mm(p[h], stack([v0[h], v1[h]])) for h in heads])
        if state is not None:
            acc = alpha * acc_old + acc
        return m_new, acc

    def plane_keys(sl, r, rows):
        return kpl[sl, r, rows, :].astype(_BF16), [vpl[sl, h, r, rows, :].astype(_BF16) for h in heads]

    def plane_q(r, rows):
        return stack([qpl[h, r, rows, :] for h in heads]).astype(_BF16)

    def load_state(r, rows):
        return stack([m_pl[h, r, rows, :] for h in heads]), stack([acc_pl[h, r, rows, :] for h in heads])

    def store_state(r, rows, m, acc):
        for h, (ah, mh) in enumerate(zip(split(acc), split(m))):
            acc_pl[h, r, rows, :] = ah
            m_pl[h, r, rows, :] = mh

    for a in range(PLANE_STRIDE):
        for r in range(N_PLANES):
            sub = lambda j, a=a: pl.ds(a + j * BLOCK * PLANE_STRIDE, BLOCK, stride=PLANE_STRIDE)
            keys = [plane_keys(prev_slot, r, sub(SUB_BLOCKS - 1))]
            for j in range(SUB_BLOCKS):
                keys.append(plane_keys(slot, r, sub(j)))
                store_state(r, sub(j), *attend(0, no_prev if j == 0 else 0, plane_q(r, sub(j)),
                                               (keys[j], keys[j + 1]), None))

    def gather(load, j):
        return stack([load(r, pl.ds(j * RUN, RUN)) for r in range(N_PLANES)])

    def token_keys(sl, j):
        k = gather(lambda r, rows: kpl[sl, r, rows, :], j).astype(_BF16)
        return k, [gather(lambda r, rows, h=h: vpl[sl, h, r, rows, :], j).astype(_BF16) for h in heads]

    masks = head_lanes(BLOCK)
    tkeys = [token_keys(prev_slot, TILE_BLOCKS - 1)]

    def token_block(j):
        tkeys.append(token_keys(slot, j))
        qq = stack([gather(lambda r, rows, h=h: qpl[h, r, rows, :], j) for h in heads]).astype(_BF16)
        state = (stack([gather(lambda r, rows, h=h: m_pl[h, r, rows, :], j) for h in heads]),
                 stack([gather(lambda r, rows, h=h: acc_pl[h, r, rows, :], j) for h in heads]))
        _, acc = attend(2, no_prev if j == 0 else 0, qq, (tkeys[j], tkeys[j + 1]), state)
        a0, a1 = split(acc)
        num = jnp.where(masks[0], a0, a1)
        den = pltpu.roll(jnp.where(masks[0], a1, a0), HEAD_DIM, axis=1)
        out = num / den
        for r in range(N_PLANES):
            o_ref[0, pl.ds(j * BLOCK + r, RUN, stride=N_PLANES), :] = out[r * RUN:(r + 1) * RUN]

    keys = [[plane_keys(prev_slot, r, pl.ds(PLANE_ROWS - BLOCK, BLOCK))] for r in range(N_PLANES)]
    for j in range(PLANE_BLOCKS):
        rows = pl.ds(j * BLOCK, BLOCK)
        for r in range(N_PLANES):
            keys[r].append(plane_keys(slot, r, rows))
            store_state(r, rows, *attend(1, no_prev if j == 0 else 0, plane_q(r, rows),
                                         (keys[r][j], keys[r][j + 1]), load_state(r, rows)))
        for jj in range(N_PLANES):
            token_block(j * N_PLANES + jj)


def _attn_call(q, k, v, bias):
    b, t, _ = q.shape
    planes = (N_PLANES, PLANE_ROWS, LANES)
    tile = pl.BlockSpec((1, ATTN_TILE, LANES), lambda i, j, s: (i, s, j))
    return pl.pallas_call(
        _attn_kernel,
        grid=(b, N_SLABS, t // ATTN_TILE),
        in_specs=[tile, tile, tile,
                  pl.BlockSpec((1,) + bias.shape[1:], lambda i, j, s: (j, 0, 0, 0, 0))],
        out_specs=tile,
        out_shape=jax.ShapeDtypeStruct((b, t, ATTN_W), _F32),
        scratch_shapes=[pltpu.VMEM((HEADS_PER_SLAB,) + planes, _F32),
                        pltpu.VMEM((2,) + planes, _F32),
                        pltpu.VMEM((2, HEADS_PER_SLAB) + planes, _F32),
                        pltpu.VMEM((HEADS_PER_SLAB,) + planes, _F32),
                        pltpu.VMEM((HEADS_PER_SLAB,) + planes, _F32)],
        compiler_params=pltpu.CompilerParams(
            dimension_semantics=("arbitrary", "arbitrary", "arbitrary"), vmem_limit_bytes=VMEM_LIMIT),
        name="attn",
    )(q, k, v, bias)


def _ffn_kernel(x_ref, yc_ref, ya_ref, p_ref, goa_ref, wout_ref, gffn_ref, wg_ref, wu_ref, fcw_ref, fcb_ref,
                wd_ref, gple_ref, wpg_ref, wpp_ref, o_ref, x1_ref, act_ref, carry_ref, work_ref):
    @pl.when(pl.program_id(1) == 0)
    def _():
        carry_ref[...] = jnp.zeros_like(carry_ref)

    ya = _rms(ya_ref[0], goa_ref[...]).astype(_BF16)
    x1 = x_ref[0] + _mm(yc_ref[0], wout_ref[0:CONV_W, :]) + _mm(ya, wout_ref[CONV_W:CONV_W + ATTN_W, :])
    x1_ref[...] = x1
    h = _rms(x1, gffn_ref[...]).astype(_BF16)

    def gate_up(c):
        cols = slice(c * FF_CHUNK, (c + 1) * FF_CHUNK)
        return _mm(h, jnp.concatenate([wg_ref[:, cols], wu_ref[:, cols]], axis=1))

    gu_next = gate_up(0)
    for c in range(N_FF_CHUNKS):
        gu = gu_next
        if c + 1 < N_FF_CHUNKS:
            gu_next = gate_up(c + 1)
        cols = slice(c * FF_CHUNK, (c + 1) * FF_CHUNK)
        gate = _causal_conv3(gu[:, :FF_CHUNK], fcw_ref[:, cols], fcb_ref[:, cols], carry_ref.at[c],
                             work_ref.at[c % 2])
        act = gate / (1.0 + jnp.exp(-gate)) * gu[:, FF_CHUNK:]
        act_ref[:, cols] = act.astype(_BF16)

    x2 = x1_ref[...] + _mm(act_ref[...], wd_ref[...])
    h3 = _rms(x2, gple_ref[...]).astype(_BF16)
    gate = 1.0 / (1.0 + jnp.exp(-_mm(h3, wpg_ref[...])))
    o_ref[0] = x2 + gate * _mm(p_ref[0].astype(_BF16), wpp_ref[...])


def _ffn_call(x, yc, ya, p, g_out_attn, w_out, g_ffn, wg, wu, fcw, fcb, wd, g_ple, wpg, wpp):
    b, t, _ = x.shape

    def const(*shape):
        return pl.BlockSpec(shape, lambda i, j: (0,) * len(shape), pipeline_mode=pl.Buffered(1))

    tile = lambda w: pl.BlockSpec((1, TOK_TILE, w), lambda i, j: (i, j, 0))
    return pl.pallas_call(
        _ffn_kernel,
        grid=(b, t // TOK_TILE),
        in_specs=[tile(D_MODEL), tile(CONV_W), tile(ATTN_W), tile(PLE_DIM),
                  const(1, ATTN_W), const(CONV_W + ATTN_W, D_MODEL), const(1, D_MODEL),
                  const(D_MODEL, D_FF), const(D_MODEL, D_FF), const(CONV_K, D_FF), const(1, D_FF),
                  const(D_FF, D_MODEL), const(1, D_MODEL),
                  const(D_MODEL, D_MODEL), const(PLE_DIM, D_MODEL)],
        out_specs=tile(D_MODEL),
        out_shape=jax.ShapeDtypeStruct((b, t, D_MODEL), _F32),
        scratch_shapes=[pltpu.VMEM((TOK_TILE, D_MODEL), _F32),
                        pltpu.VMEM((TOK_TILE, D_FF), _BF16),
                        pltpu.VMEM((N_FF_CHUNKS, SUBLANES, FF_CHUNK), _F32),
                        pltpu.VMEM((2, SUBLANES + TOK_TILE, FF_CHUNK), _F32)],
        compiler_params=pltpu.CompilerParams(
            dimension_semantics=("arbitrary", "arbitrary"), vmem_limit_bytes=VMEM_LIMIT),
        name="ffn",
    )(x, yc, ya, p, g_out_attn, w_out, g_ffn, wg, wu, fcw, fcb, wd, g_ple, wpg, wpp)


def kernel(x, p, g_mix, w_in, conv_w, conv_b, q_norm_g, k_norm_g, g_out_conv, g_out_attn, w_out, g_ffn,
           w_gate, w_up, ffn_conv_w, ffn_conv_b, w_down, g_ple, w_ple_gate, w_ple_proj):
    depth = w_in.shape[0]
    row = lambda a: a.reshape(1, -1)
    bias = jnp.asarray(_bias_tables())
    for i in range(depth):
        (yc, q, k, v), (wg, wu, wd, wout, wpg, wpp) = _proj_call(
            x, row(g_mix[i]), w_in[i].astype(_BF16), conv_w[i], row(conv_b[i]),
            row(jnp.tile(q_norm_g[i], N_HEADS)), row(jnp.tile(k_norm_g[i], N_HEADS)),
            row(g_out_conv[i]), (w_gate[i], w_up[i], w_down[i], w_out[i], w_ple_gate[i], w_ple_proj[i]))
        ya = _attn_call(q, k, v, bias)
        x = _ffn_call(
            x, yc, ya, p[i], row(g_out_attn[i]), wout, row(g_ffn[i]),
            wg, wu, ffn_conv_w[i], row(ffn_conv_b[i]), wd, row(g_ple[i]), wpg, wpp)
    return x
```

```python
import numpy as np
import jax
import jax.numpy as jnp
from jax import lax
from jax.experimental import pallas as pl
from jax.experimental.pallas import tpu as pltpu

D_MODEL = 1024
CONV_W = 512
N_HEADS = 8
HEAD_DIM = 64
ATTN_W = N_HEADS * HEAD_DIM
CONV_K = 3
D_FF = 2816
PLE_DIM = 256
DILATED_PAIRS = ((128, 1), (512, 4), (2048, 16))
BLOCK = 128
EPS = 1e-6

LANES = 128
SUBLANES = 8
HEADS_PER_SLAB = LANES // HEAD_DIM
N_SLABS = ATTN_W // LANES

PROJ_TILE = 1024
TOK_TILE = 512
ATTN_TILE = 2048
FF_CHUNK = 256
N_FF_CHUNKS = D_FF // FF_CHUNK
NEG = -1e30
LOG2E = 1.4426950408889634
VMEM_LIMIT = 56 * 1024 * 1024

_F32 = jnp.float32
_BF16 = jnp.bfloat16


def _rms(x, g):
    ms = jnp.mean(x * x, axis=-1, keepdims=True)
    return x * lax.rsqrt(ms + EPS) * g


def _mm(a, b):
    return jnp.dot(a, b, preferred_element_type=_F32)


def _causal_conv3(u, w, b, carry_ref, work_ref):
    n = u.shape[0]
    work_ref[0:SUBLANES, :] = carry_ref[...]
    work_ref[SUBLANES:SUBLANES + n, :] = u
    u1 = work_ref[SUBLANES - 1:SUBLANES - 1 + n, :]
    u2 = work_ref[SUBLANES - 2:SUBLANES - 2 + n, :]
    carry_ref[...] = work_ref[n:n + SUBLANES, :]
    return u2 * w[0:1, :] + u1 * w[1:2, :] + u * w[2:3, :] + b


N_PROJ_OUTPUTS = 4


def _proj_kernel(x_ref, gmix_ref, win_ref, cw_ref, cb_ref, gq_ref, gk_ref, goc_ref, *refs):
    n_w = (len(refs) - N_PROJ_OUTPUTS - 2) // 2
    w_refs, (yc_ref, q_ref, k_ref, v_ref) = refs[:n_w], refs[n_w:n_w + N_PROJ_OUTPUTS]
    w_out_refs, (carry_ref, work_ref) = refs[n_w + N_PROJ_OUTPUTS:-2], refs[-2:]

    @pl.when(pl.program_id(1) == 0)
    def _():
        carry_ref[...] = jnp.zeros_like(carry_ref)

    h = _rms(x_ref[0], gmix_ref[...]).astype(_BF16)

    def proj(c):
        return _mm(h, win_ref[:, c * CONV_W:(c + 1) * CONV_W])

    for src, dst in zip(w_refs, w_out_refs):
        dst[...] = src[...].astype(_BF16)

    u = proj(1) * proj(2)
    conv = _causal_conv3(u, cw_ref[...], cb_ref[...], carry_ref, work_ref)
    yc = proj(0) * conv
    yc_ref[0] = _rms(yc, goc_ref[...]).astype(_BF16)

    lane = lax.broadcasted_iota(jnp.int32, (PROJ_TILE, LANES), 1)

    def head_norm(z, g_ref):
        zz = z * z
        slabs = []
        for c in range(N_SLABS):
            blk = zz[:, c * LANES:(c + 1) * LANES]
            ms = jnp.zeros_like(blk)
            for h in range(HEADS_PER_SLAB):
                own = (lane >= h * HEAD_DIM) & (lane < (h + 1) * HEAD_DIM)
                ms = jnp.where(own, jnp.sum(jnp.where(own, blk, 0.0), axis=-1, keepdims=True), ms)
            slabs.append(ms * (1.0 / HEAD_DIM))
        ms = jnp.concatenate(slabs, axis=-1)
        return z * lax.rsqrt(ms + EPS) * g_ref[...]

    q_ref[0] = head_norm(proj(3), gq_ref) * (HEAD_DIM ** -0.5 * LOG2E)
    k_ref[0] = head_norm(proj(4), gk_ref)
    v_ref[0] = proj(5)


def _proj_call(x, g_mix, w_in, conv_w, conv_b, gq, gk, g_out_conv, ffn_weights):
    b, t, _ = x.shape
    steps_per_seq = t // PROJ_TILE
    n_steps = b * steps_per_seq
    const = lambda *shape: pl.BlockSpec(shape, lambda i, j: (0,) * len(shape), pipeline_mode=pl.Buffered(1))
    tile = lambda w: pl.BlockSpec((1, PROJ_TILE, w), lambda i, j: (i, j, 0))

    def row_blocks(w):
        rows, cols = w.shape
        block = next(r for r in range(2 * SUBLANES, rows + 1, 2 * SUBLANES)
                     if rows % r == 0 and rows // r <= n_steps)
        last = rows // block - 1
        return pl.BlockSpec((block, cols), lambda i, j: (jnp.minimum(i * steps_per_seq + j, last), 0))

    w_specs = [row_blocks(w) for w in ffn_weights]
    outs = pl.pallas_call(
        _proj_kernel,
        grid=(b, steps_per_seq),
        in_specs=[tile(D_MODEL), const(1, D_MODEL), const(D_MODEL, 3 * CONV_W + 3 * ATTN_W),
                  const(CONV_K, CONV_W), const(1, CONV_W), const(1, ATTN_W), const(1, ATTN_W),
                  const(1, CONV_W)] + w_specs,
        out_specs=[tile(CONV_W), tile(ATTN_W), tile(ATTN_W), tile(ATTN_W)] + w_specs,
        out_shape=[jax.ShapeDtypeStruct((b, t, CONV_W), _BF16),
                   jax.ShapeDtypeStruct((b, t, ATTN_W), _F32),
                   jax.ShapeDtypeStruct((b, t, ATTN_W), _F32),
                   jax.ShapeDtypeStruct((b, t, ATTN_W), _F32)]
                  + [jax.ShapeDtypeStruct(w.shape, _BF16) for w in ffn_weights],
        scratch_shapes=[pltpu.VMEM((SUBLANES, CONV_W), _F32),
                        pltpu.VMEM((SUBLANES + PROJ_TILE, CONV_W), _F32)],
        compiler_params=pltpu.CompilerParams(
            dimension_semantics=("arbitrary", "arbitrary"), vmem_limit_bytes=VMEM_LIMIT),
        name="proj",
    )(x, g_mix, w_in, conv_w, conv_b, gq, gk, g_out_conv, *ffn_weights)
    return outs[:N_PROJ_OUTPUTS], outs[N_PROJ_OUTPUTS:]


_BRANCHES = tuple(sorted(DILATED_PAIRS, key=lambda wd: -wd[1]))
_STACK = HEADS_PER_SLAB * BLOCK
assert HEADS_PER_SLAB == 2

_DILATIONS = tuple(d for _, d in _BRANCHES)
assert _DILATIONS == (16, 4, 1)
N_PLANES = _DILATIONS[1]
PLANE_ROWS = ATTN_TILE // N_PLANES
PLANE_STRIDE = _DILATIONS[0] // N_PLANES
PLANE_BLOCKS = PLANE_ROWS // BLOCK
SUB_BLOCKS = PLANE_BLOCKS // PLANE_STRIDE
TILE_BLOCKS = ATTN_TILE // BLOCK
RUN = BLOCK // N_PLANES
_PLANE_ORDER = np.array([N_PLANES * (p % RUN) + p // RUN for p in range(BLOCK)])


def _bias_tables():
    qi = np.arange(BLOCK)[:, None] + BLOCK
    kj = np.arange(2 * BLOCK)[None, :]
    step = (qi - kj).astype(np.float64)
    tabs = np.zeros((N_SLABS, len(_BRANCHES), 2, _STACK, 2 * BLOCK), np.float32)
    for bi, (window, dilation) in enumerate(_BRANCHES):
        band = (step >= 0) & (step <= window // dilation)
        for head in range(N_HEADS):
            slope = 2.0 ** (-8.0 * (head + 1) / N_HEADS)
            slab, h = divmod(head, HEADS_PER_SLAB)
            for variant, keep in enumerate((band, band & (kj >= BLOCK))):
                tab = np.where(keep, -slope * step * dilation * LOG2E, NEG)
                if dilation == 1:
                    tab = tab[_PLANE_ORDER][:, np.concatenate([_PLANE_ORDER, BLOCK + _PLANE_ORDER])]
                tabs[slab, bi, variant, h * BLOCK:(h + 1) * BLOCK] = tab
    return tabs


def _attn_kernel(q_ref, k_ref, v_ref, bias_ref, o_ref, qpl, kpl, vpl, acc_pl, m_pl):
    tile = pl.program_id(2)
    first_tile = tile == 0
    slot = tile % 2
    prev_slot = jnp.where(first_tile, slot, 1 - slot)
    no_prev = first_tile.astype(jnp.int32)
    heads = range(HEADS_PER_SLAB)

    def head_lanes(n_rows):
        lane = lax.broadcasted_iota(jnp.int32, (n_rows, LANES), 1)
        return [(lane >= h * HEAD_DIM) & (lane < (h + 1) * HEAD_DIM) for h in heads]

    def stack(parts):
        return jnp.concatenate(parts, axis=0)

    def split(a):
        return [a[h * BLOCK:(h + 1) * BLOCK] for h in heads]

    def expand_q(q, masks):
        return [jnp.where(masks[h], q, 0.0) for h in heads]

    def expand_v(v, masks):
        return [jnp.where(masks[h], v, 1.0) for h in heads]

    plane_masks = head_lanes(BLOCK)
    for r in range(N_PLANES):
        for c in range(PLANE_BLOCKS):
            src = pl.ds(r + c * BLOCK * N_PLANES, BLOCK, stride=N_PLANES)
            dst = pl.ds(c * BLOCK, BLOCK)
            qe = expand_q(q_ref[0, src, :], plane_masks)
            ve = expand_v(v_ref[0, src, :], plane_masks)
            kpl[slot, r, dst, :] = k_ref[0, src, :]
            for h in heads:
                qpl[h, r, dst, :] = qe[h]
                vpl[slot, h, r, dst, :] = ve[h]

    def attend(bi, variant, qq, keys, state):
        (k0, v0), (k1, v1) = keys
        s = lax.dot_general(qq, stack([k0, k1]), (((1,), (1,)), ((), ())), preferred_element_type=_F32)
        s = s + bias_ref[0, bi, variant]
        m_new = jnp.broadcast_to(jnp.max(s, axis=-1, keepdims=True), (_STACK, LANES))
        if state is not None:
            m_old, acc_old = state
            m_new = jnp.maximum(m_old, m_new)
            alpha = jnp.exp2(m_old - m_new)
        p = split(jnp.exp2((s - jnp.concatenate([m_new, m_new], axis=1)).astype(_BF16)))
        acc = stack([_mm(p[h], stack([v0[h], v1[h]])) for h in heads])
        if state is not None:
            acc = alpha * acc_old + acc
        return m_new, acc

    def plane_keys(sl, r, rows):
        return kpl[sl, r, rows, :].astype(_BF16), [vpl[sl, h, r, rows, :].astype(_BF16) for h in heads]

    def plane_q(r, rows):
        return stack([qpl[h, r, rows, :] for h in heads]).astype(_BF16)

    def load_state(r, rows):
        return stack([m_pl[h, r, rows, :] for h in heads]), stack([acc_pl[h, r, rows, :] for h in heads])

    def store_state(r, rows, m, acc):
        for h, (ah, mh) in enumerate(zip(split(acc), split(m))):
            acc_pl[h, r, rows, :] = ah
            m_pl[h, r, rows, :] = mh

    for a in range(PLANE_STRIDE):
        for r in range(N_PLANES):
            sub = lambda j, a=a: pl.ds(a + j * BLOCK * PLANE_STRIDE, BLOCK, stride=PLANE_STRIDE)
            keys = [plane_keys(prev_slot, r, sub(SUB_BLOCKS - 1))]
            for j in range(SUB_BLOCKS):
                keys.append(plane_keys(slot, r, sub(j)))
                store_state(r, sub(j), *attend(0, no_prev if j == 0 else 0, plane_q(r, sub(j)),
                                               (keys[j], keys[j + 1]), None))

    def gather(load, j):
        return stack([load(r, pl.ds(j * RUN, RUN)) for r in range(N_PLANES)])

    def token_keys(sl, j):
        k = gather(lambda r, rows: kpl[sl, r, rows, :], j).astype(_BF16)
        return k, [gather(lambda r, rows, h=h: vpl[sl, h, r, rows, :], j).astype(_BF16) for h in heads]

    masks = head_lanes(BLOCK)
    tkeys = [token_keys(prev_slot, TILE_BLOCKS - 1)]

    def token_block(j):
        tkeys.append(token_keys(slot, j))
        qq = stack([gather(lambda r, rows, h=h: qpl[h, r, rows, :], j) for h in heads]).astype(_BF16)
        state = (stack([gather(lambda r, rows, h=h: m_pl[h, r, rows, :], j) for h in heads]),
                 stack([gather(lambda r, rows, h=h: acc_pl[h, r, rows, :], j) for h in heads]))
        _, acc = attend(2, no_prev if j == 0 else 0, qq, (tkeys[j], tkeys[j + 1]), state)
        a0, a1 = split(acc)
        num = jnp.where(masks[0], a0, a1)
        den = pltpu.roll(jnp.where(masks[0], a1, a0), HEAD_DIM, axis=1)
        out = num / den
        for r in range(N_PLANES):
            o_ref[0, pl.ds(j * BLOCK + r, RUN, stride=N_PLANES), :] = out[r * RUN:(r + 1) * RUN]

    keys = [[plane_keys(prev_slot, r, pl.ds(PLANE_ROWS - BLOCK, BLOCK))] for r in range(N_PLANES)]
    for j in range(PLANE_BLOCKS):
        rows = pl.ds(j * BLOCK, BLOCK)
        for r in range(N_PLANES):
            keys[r].append(plane_keys(slot, r, rows))
            store_state(r, rows, *attend(1, no_prev if j == 0 else 0, plane_q(r, rows),
                                         (keys[r][j], keys[r][j + 1]), load_state(r, rows)))
        for jj in range(N_PLANES):
            token_block(j * N_PLANES + jj)


def _attn_call(q, k, v, bias):
    b, t, _ = q.shape
    planes = (N_PLANES, PLANE_ROWS, LANES)
    tile = pl.BlockSpec((1, ATTN_TILE, LANES), lambda i, j, s: (i, s, j))
    return pl.pallas_call(
        _attn_kernel,
        grid=(b, N_SLABS, t // ATTN_TILE),
        in_specs=[tile, tile, tile,
                  pl.BlockSpec((1,) + bias.shape[1:], lambda i, j, s: (j, 0, 0, 0, 0))],
        out_specs=tile,
        out_shape=jax.ShapeDtypeStruct((b, t, ATTN_W), _F32),
        scratch_shapes=[pltpu.VMEM((HEADS_PER_SLAB,) + planes, _F32),
                        pltpu.VMEM((2,) + planes, _F32),
                        pltpu.VMEM((2, HEADS_PER_SLAB) + planes, _F32),
                        pltpu.VMEM((HEADS_PER_SLAB,) + planes, _F32),
                        pltpu.VMEM((HEADS_PER_SLAB,) + planes, _F32)],
        compiler_params=pltpu.CompilerParams(
            dimension_semantics=("arbitrary", "arbitrary", "arbitrary"), vmem_limit_bytes=VMEM_LIMIT),
        name="attn",
    )(q, k, v, bias)


def _ffn_kernel(x_ref, yc_ref, ya_ref, p_ref, goa_ref, wout_ref, gffn_ref, wg_ref, wu_ref, fcw_ref, fcb_ref,
                wd_ref, gple_ref, wpg_ref, wpp_ref, o_ref, x1_ref, act_ref, carry_ref, work_ref):
    @pl.when(pl.program_id(1) == 0)
    def _():
        carry_ref[...] = jnp.zeros_like(carry_ref)

    ya = _rms(ya_ref[0], goa_ref[...]).astype(_BF16)
    x1 = x_ref[0] + _mm(yc_ref[0], wout_ref[0:CONV_W, :]) + _mm(ya, wout_ref[CONV_W:CONV_W + ATTN_W, :])
    x1_ref[...] = x1
    h = _rms(x1, gffn_ref[...]).astype(_BF16)

    def gate_up(c):
        cols = slice(c * FF_CHUNK, (c + 1) * FF_CHUNK)
        return _mm(h, jnp.concatenate([wg_ref[:, cols], wu_ref[:, cols]], axis=1))

    gu_next = gate_up(0)
    for c in range(N_FF_CHUNKS):
        gu = gu_next
        if c + 1 < N_FF_CHUNKS:
            gu_next = gate_up(c + 1)
        cols = slice(c * FF_CHUNK, (c + 1) * FF_CHUNK)
        gate = _causal_conv3(gu[:, :FF_CHUNK], fcw_ref[:, cols], fcb_ref[:, cols], carry_ref.at[c],
                             work_ref.at[c % 2])
        act = gate / (1.0 + jnp.exp(-gate)) * gu[:, FF_CHUNK:]
        act_ref[:, cols] = act.astype(_BF16)

    x2 = x1_ref[...] + _mm(act_ref[...], wd_ref[...])
    h3 = _rms(x2, gple_ref[...]).astype(_BF16)
    gate = 1.0 / (1.0 + jnp.exp(-_mm(h3, wpg_ref[...])))
    o_ref[0] = x2 + gate * _mm(p_ref[0].astype(_BF16), wpp_ref[...])


def _ffn_call(x, yc, ya, p, g_out_attn, w_out, g_ffn, wg, wu, fcw, fcb, wd, g_ple, wpg, wpp):
    b, t, _ = x.shape

    def const(*shape):
        return pl.BlockSpec(shape, lambda i, j: (0,) * len(shape), pipeline_mode=pl.Buffered(1))

    tile = lambda w: pl.BlockSpec((1, TOK_TILE, w), lambda i, j: (i, j, 0))
    return pl.pallas_call(
        _ffn_kernel,
        grid=(b, t // TOK_TILE),
        in_specs=[tile(D_MODEL), tile(CONV_W), tile(ATTN_W), tile(PLE_DIM),
                  const(1, ATTN_W), const(CONV_W + ATTN_W, D_MODEL), const(1, D_MODEL),
                  const(D_MODEL, D_FF), const(D_MODEL, D_FF), const(CONV_K, D_FF), const(1, D_FF),
                  const(D_FF, D_MODEL), const(1, D_MODEL),
                  const(D_MODEL, D_MODEL), const(PLE_DIM, D_MODEL)],
        out_specs=tile(D_MODEL),
        out_shape=jax.ShapeDtypeStruct((b, t, D_MODEL), _F32),
        scratch_shapes=[pltpu.VMEM((TOK_TILE, D_MODEL), _F32),
                        pltpu.VMEM((TOK_TILE, D_FF), _BF16),
                        pltpu.VMEM((N_FF_CHUNKS, SUBLANES, FF_CHUNK), _F32),
                        pltpu.VMEM((2, SUBLANES + TOK_TILE, FF_CHUNK), _F32)],
        compiler_params=pltpu.CompilerParams(
            dimension_semantics=("arbitrary", "arbitrary"), vmem_limit_bytes=VMEM_LIMIT),
        name="ffn",
    )(x, yc, ya, p, g_out_attn, w_out, g_ffn, wg, wu, fcw, fcb, wd, g_ple, wpg, wpp)


def kernel(x, p, g_mix, w_in, conv_w, conv_b, q_norm_g, k_norm_g, g_out_conv, g_out_attn, w_out, g_ffn,
           w_gate, w_up, ffn_conv_w, ffn_conv_b, w_down, g_ple, w_ple_gate, w_ple_proj):
    depth = w_in.shape[0]
    row = lambda a: a.reshape(1, -1)
    bias = jnp.asarray(_bias_tables())
    for i in range(depth):
        (yc, q, k, v), (wg, wu, wd, wout, wpg, wpp) = _proj_call(
            x, row(g_mix[i]), w_in[i].astype(_BF16), conv_w[i], row(conv_b[i]),
            row(jnp.tile(q_norm_g[i], N_HEADS)), row(jnp.tile(k_norm_g[i], N_HEADS)),
            row(g_out_conv[i]), (w_gate[i], w_up[i], w_down[i], w_out[i], w_ple_gate[i], w_ple_proj[i]))
        ya = _attn_call(q, k, v, bias)
        x = _ffn_call(
            x, yc, ya, p[i], row(g_out_attn[i]), wout, row(g_ffn[i]),
            wg, wu, ffn_conv_w[i], row(ffn_conv_b[i]), wd, row(g_ple[i]), wpg, wpp)
    return x
```
